```python
import jax
import jax.numpy as jnp
from jax import lax
import numpy as np

D_MODEL = 1024
BATCH = 32
SEQ = 256
DEPTH = 4
DEC_BATCH = 4
DEC_SEQ = 2048
PAST_LEN = 512

GRID_W = 64
N_REC = (DEPTH + 1) // 2
N_NA = DEPTH // 2
A_HEADS = 4
A_DK = 128
A_DV = 128
A_QK = A_HEADS * A_DK
A_V = A_HEADS * A_DV
B_WIDTH = D_MODEL - A_V
CONV_W = 31
GLA_CHUNK = 16
C_HEADS = 16
C_HD = D_MODEL // C_HEADS
WIN_H = 8
WIN_W = 16
QBLK = 128
PEER_HEADS = 8
PEER_QDIM = 256
N_KEYS = 128
N_EXPERTS = N_KEYS * N_KEYS
PEER_TOPK = 16
PEER_BLK = 128
N_MOD = 6
EPS = 1e-6

kernel_name = 'hybrid_hgrn2_conformer_natten_peer_flow_step'


def _rms(x):
    xf = x.astype(jnp.float32)
    return xf * lax.rsqrt(jnp.mean(xf * xf, axis=-1, keepdims=True) + EPS)


def _layernorm(x, g, b):
    xf = x.astype(jnp.float32)
    xc = xf - jnp.mean(xf, axis=-1, keepdims=True)
    var = jnp.mean(xc * xc, axis=-1, keepdims=True)
    return (xc * lax.rsqrt(var + EPS) * g + b).astype(x.dtype)


def _modulation(cond, w_mod, b_mod):
    m = jax.nn.silu(cond) @ w_mod + b_mod
    return m.reshape(cond.shape[0], N_MOD, D_MODEL)


def _mod_norm(x, g, shift, scale):
    h = _rms(x) * g * (1.0 + scale[:, None, :]) + shift[:, None, :]
    return h.astype(x.dtype)


def _split_heads(t, n_heads):
    b, s, w = t.shape
    return t.reshape(b, s, n_heads, w // n_heads).transpose(0, 2, 1, 3)


def _merge_heads(t):
    b, h, s, d = t.shape
    return t.transpose(0, 2, 1, 3).reshape(b, s, h * d)


def _forget(zf, lbd):
    lbd = lbd[None, :, None, :]
    logf = jnp.logaddexp(jnp.log1p(-lbd) + jax.nn.log_sigmoid(zf), jnp.log(lbd))
    k = (1.0 - lbd) * jax.nn.sigmoid(-zf)
    return logf, k


def _gla_chunked(q, k, v, logf, s0):
    b_, h_, t_, dk = q.shape
    n = t_ // GLA_CHUNK
    rs = lambda a: a.reshape(b_, h_, n, GLA_CHUNK, a.shape[-1])
    q, k, v, logf = rs(q), rs(k), rs(v), rs(logf)
    bcum = jnp.cumsum(logf, axis=3)
    causal = jnp.tril(jnp.ones((GLA_CHUNK, GLA_CHUNK), dtype=bool))
    diff = bcum[:, :, :, :, None, :] - bcum[:, :, :, None, :, :]
    decay = jnp.exp(jnp.where(causal[:, :, None], diff, -jnp.inf))
    attn = jnp.einsum('bhntd,bhnsd,bhntsd->bhnts', q, k, decay)
    o_intra = jnp.einsum('bhnts,bhnsv->bhntv', attn, v)
    b_last = bcum[:, :, :, -1:, :]
    q_dec = q * jnp.exp(bcum)
    kv = jnp.einsum('bhnsd,bhnsv->bhndv', k * jnp.exp(b_last - bcum), v)
    a_chunk = jnp.exp(b_last[:, :, :, 0, :])

    def step(s, xs):
        qd, a, kvc = xs
        o = jnp.einsum('bhtd,bhdv->bhtv', qd, s)
        return a[..., None] * s + kvc, o

    s_final, o_inter = lax.scan(step, s0.astype(jnp.float32),
                                (jnp.moveaxis(q_dec, 2, 0), jnp.moveaxis(a_chunk, 2, 0), jnp.moveaxis(kv, 2, 0)))
    o = o_intra + jnp.moveaxis(o_inter, 0, 2)
    return o.reshape(b_, h_, t_, v.shape[-1]), s_final


def _conv_module(ua, ug, cv_w, cv_b, ln_g, ln_b):
    h = ua * jax.nn.sigmoid(ug)
    ch = h.shape[-1]
    h = lax.conv_general_dilated(h, cv_w[:, None, :].astype(h.dtype), window_strides=(1,),
                                 padding=[(CONV_W // 2, CONV_W // 2)],
                                 dimension_numbers=('NWC', 'WIO', 'NWC'),
                                 feature_group_count=ch) + cv_b
    return jax.nn.silu(_layernorm(h, ln_g, ln_b))


def _rec_conv_mixer(h, s0_f, s0_b, w_in, lb, gn_g, cv_w, cv_b, ln_g, ln_b, w_out):
    f32 = jnp.float32
    cuts = [A_QK, 2 * A_QK, 3 * A_QK, 3 * A_QK + A_V, 3 * A_QK + 2 * A_V, 3 * A_QK + 2 * A_V + B_WIDTH]
    zq, zf_f, zf_b, zi, zg, ua, ug = jnp.split(h @ w_in, cuts, axis=-1)
    q = _split_heads(zq, A_HEADS).astype(f32) * (A_DK ** -0.5)
    v = _split_heads(zi, A_HEADS).astype(f32)
    logf_f, k_f = _forget(_split_heads(zf_f, A_HEADS).astype(f32), lb[0])
    logf_b, k_b = _forget(_split_heads(zf_b, A_HEADS).astype(f32), lb[1])
    o_f, s_f = _gla_chunked(q, k_f, v, logf_f, s0_f)
    rev = lambda a: jnp.flip(a, axis=2)
    o_r, s_b = _gla_chunked(rev(q), rev(k_b), rev(v), rev(logf_b), s0_b)
    o = o_f + rev(o_r)
    gate = jax.nn.silu(_split_heads(zg, A_HEADS).astype(f32))
    o_a = _merge_heads(_rms(o) * gn_g * gate).astype(h.dtype)
    o_c = _conv_module(ua, ug, cv_w, cv_b, ln_g, ln_b)
    return jnp.concatenate([o_a, o_c], axis=-1) @ w_out, s_f, s_b


def _na_qkv(h, w_qkv, qk_g):
    zq, zk, zv = jnp.split(h @ w_qkv, 3, axis=-1)
    q = (_rms(_split_heads(zq, C_HEADS)) * qk_g[0]).astype(h.dtype)
    k = (_rms(_split_heads(zk, C_HEADS)) * qk_g[1]).astype(h.dtype)
    return q, k, _split_heads(zv, C_HEADS)


def _ctx_attention(q, k, v):
    b, hh, s, d = q.shape
    qb = jnp.moveaxis(q.reshape(b, hh, s // QBLK, QBLK, d), 2, 0)

    def block(qi):
        sc = jnp.einsum('bhqd,bhkd->bhqk', qi, k).astype(jnp.float32) * (d ** -0.5)
        p = jax.nn.softmax(sc, axis=-1).astype(v.dtype)
        return jnp.einsum('bhqk,bhkd->bhqd', p, v)

    o = lax.map(block, qb)
    return jnp.moveaxis(o, 0, 2).reshape(b, hh, s, d)


def _na_latent(q, k, v, k_ctx, v_ctx, rpb):
    b, hh, t, d = q.shape
    rows = t // GRID_W
    kh = min(WIN_H, rows)
    r = jnp.arange(rows)
    r0 = jnp.clip(r - kh // 2, 0, rows - kh)
    row_idx = r0[:, None] + jnp.arange(kh)[None, :]
    dr = row_idx - r[:, None] + (WIN_H - 1)
    cq = jnp.arange(GRID_W)
    c0 = jnp.clip(cq - WIN_W // 2, 0, GRID_W - WIN_W)
    in_win = (cq[None, :] >= c0[:, None]) & (cq[None, :] < c0[:, None] + WIN_W)
    dc = jnp.clip(cq[None, :] - cq[:, None] + (WIN_W - 1), 0, 2 * WIN_W - 2)
    bias = rpb[:, dr[:, None, :, None], dc[None, :, None, :]].astype(jnp.float32)
    bias = jnp.where(in_win[None, None, :, None, :], bias, -jnp.inf)
    qg = q.reshape(b, hh, rows, GRID_W, d)
    kg = k.reshape(b, hh, rows, GRID_W, d)[:, :, row_idx]
    vg = v.reshape(b, hh, rows, GRID_W, d)[:, :, row_idx]
    scale = d ** -0.5
    s_loc = jnp.einsum('bhrcd,bhrijd->bhrcij', qg, kg).astype(jnp.float32) * scale + bias[None]
    s_ctx = jnp.einsum('bhrcd,bhpd->bhrcp', qg, k_ctx).astype(jnp.float32) * scale
    n_loc = kh * GRID_W
    logits = jnp.concatenate([s_loc.reshape(b, hh, rows, GRID_W, n_loc), s_ctx], axis=-1)
    p = jax.nn.softmax(logits, axis=-1).astype(v.dtype)
    p_loc = p[..., :n_loc].reshape(b, hh, rows, GRID_W, kh, GRID_W)
    o = (jnp.einsum('bhrcij,bhrijd->bhrcd', p_loc, vg)
         + jnp.einsum('bhrcp,bhpd->bhrcd', p[..., n_loc:], v_ctx))
    return o.reshape(b, hh, t, d)


def _peer(h, w_q, sub_keys, u, v):
    b, t, dm = h.shape
    hb = h.reshape(-1, PEER_BLK, dm)

    def block(xb):
        n = xb.shape[0]
        qh = (xb @ w_q).reshape(n, PEER_HEADS, 2, PEER_QDIM // 2)
        s = jnp.einsum('nhpd,hpkd->nhpk', qh, sub_keys).astype(jnp.float32)
        sv, si = lax.top_k(s, PEER_TOPK)
        cand = sv[:, :, 0, :, None] + sv[:, :, 1, None, :]
        cidx = si[:, :, 0, :, None] * N_KEYS + si[:, :, 1, None, :]
        best, pos = lax.top_k(cand.reshape(n, PEER_HEADS, -1), PEER_TOPK)
        eidx = jnp.take_along_axis(cidx.reshape(n, PEER_HEADS, -1), pos, axis=-1)
        gate = jax.nn.softmax(best, axis=-1)
        ue = jnp.take(u, eidx, axis=0)
        act = jax.nn.gelu(jnp.einsum('nd,nhkd->nhk', xb, ue).astype(jnp.float32))
        ve = jnp.take(v, eidx, axis=0)
        return jnp.einsum('nhk,nhkd->nd', (gate * act).astype(xb.dtype), ve)

    return lax.map(block, hb).reshape(b, t, dm)


def setup_inputs(seed: int = 0) -> dict:
    key = jax.random.key(seed)
    ks = jax.random.split(key, 27)
    f32 = jnp.float32

    def nrm(k, shape, scale):
        return jax.random.normal(k, shape, f32) * scale

    n_in = 3 * A_QK + 2 * A_V + 2 * B_WIDTH
    return {
        'x_prompt': nrm(ks[0], (BATCH, SEQ, D_MODEL), 1.0),
        'x_sample': nrm(ks[1], (DEC_BATCH, DEC_SEQ, D_MODEL), 1.0),
        'cache_k': nrm(ks[2], (DEC_BATCH, N_NA, C_HEADS, PAST_LEN, C_HD), 1.0),
        'cache_v': nrm(ks[3], (DEC_BATCH, N_NA, C_HEADS, PAST_LEN, C_HD), 1.0),
        'state_fwd': nrm(ks[4], (DEC_BATCH, N_REC, A_HEADS, A_DK, A_DV), 0.5),
        'state_bwd': nrm(ks[5], (DEC_BATCH, N_REC, A_HEADS, A_DK, A_DV), 0.5),
        'c': nrm(ks[6], (DEC_BATCH, D_MODEL), 1.0),
        'c_ctx': nrm(ks[7], (D_MODEL,), 1.0),
        'norm_g': 1.0 + nrm(ks[8], (DEPTH, 2, D_MODEL), 0.01),
        'w_mod': nrm(ks[9], (DEPTH, D_MODEL, N_MOD * D_MODEL), 0.5 * D_MODEL ** -0.5),
        'b_mod': nrm(ks[10], (DEPTH, N_MOD * D_MODEL), 0.02),
        'w_in_ab': nrm(ks[11], (N_REC, D_MODEL, n_in), D_MODEL ** -0.5),
        'hg_lb': nrm(ks[12], (N_REC, 2, A_QK), 1.0),
        'hg_norm_g': 1.0 + nrm(ks[13], (N_REC, A_DV), 0.01),
        'cv_w': nrm(ks[14], (N_REC, CONV_W, B_WIDTH), CONV_W ** -0.5),
        'cv_b': nrm(ks[15], (N_REC, B_WIDTH), 0.02),
        'cv_ln_g': 1.0 + nrm(ks[16], (N_REC, B_WIDTH), 0.01),
        'cv_ln_b': nrm(ks[17], (N_REC, B_WIDTH), 0.02),
        'w_out_ab': nrm(ks[18], (N_REC, A_V + B_WIDTH, D_MODEL), (A_V + B_WIDTH) ** -0.5),
        'w_qkv': nrm(ks[19], (N_NA, D_MODEL, 3 * C_HEADS * C_HD), D_MODEL ** -0.5),
        'qk_g': 1.0 + nrm(ks[20], (N_NA, 2, C_HD), 0.01),
        'rpb': nrm(ks[21], (N_NA, C_HEADS, 2 * WIN_H - 1, 2 * WIN_W - 1), 0.1),
        'w_out_na': nrm(ks[22], (N_NA, C_HEADS * C_HD, D_MODEL), (C_HEADS * C_HD) ** -0.5),
        'peer_wq': nrm(ks[23], (DEPTH, D_MODEL, PEER_HEADS * PEER_QDIM), D_MODEL ** -0.5),
        'peer_keys': nrm(ks[24], (DEPTH, PEER_HEADS, 2, N_KEYS, PEER_QDIM // 2), (PEER_QDIM // 2) ** -0.5),
        'peer_u': nrm(ks[25], (DEPTH, N_EXPERTS, D_MODEL), D_MODEL ** -0.5),
        'peer_v': nrm(ks[26], (DEPTH, N_EXPERTS, D_MODEL), PEER_HEADS ** -0.5),
    }


def reference(x_prompt, x_sample, cache_k, cache_v, state_fwd, state_bwd, c, c_ctx,
              norm_g, w_mod, b_mod, w_in_ab, hg_lb, hg_norm_g, cv_w, cv_b, cv_ln_g, cv_ln_b, w_out_ab,
              w_qkv, qk_g, rpb, w_out_na, peer_wq, peer_keys, peer_u, peer_v):
    f32 = jnp.float32
    lb_all = jnp.cumsum(jax.nn.softmax(hg_lb.astype(f32), axis=0), axis=0)
    lb_all = lb_all - lb_all[:1]
    yp, ys = x_prompt, x_sample
    bp = x_prompt.shape[0]
    new_k, new_v, new_sf, new_sb = [], [], [], []
    for l in range(DEPTH):
        j = l // 2
        mp = _modulation(c_ctx[None, :], w_mod[l], b_mod[l])
        ms = _modulation(c, w_mod[l], b_mod[l])
        hp = _mod_norm(yp, norm_g[l, 0], mp[:, 0], mp[:, 1])
        hs = _mod_norm(ys, norm_g[l, 0], ms[:, 0], ms[:, 1])
        if l % 2 == 0:
            lb = lb_all[j].reshape(2, A_HEADS, A_DK)
            prm = (w_in_ab[j], lb, hg_norm_g[j], cv_w[j], cv_b[j], cv_ln_g[j], cv_ln_b[j], w_out_ab[j])
            zeros = jnp.zeros((bp, A_HEADS, A_DK, A_DV), f32)
            op, sf, sb = _rec_conv_mixer(hp, zeros, zeros, *prm)
            os_, _, _ = _rec_conv_mixer(hs, state_fwd[:, j], state_bwd[:, j], *prm)
            new_sf.append(sf)
            new_sb.append(sb)
        else:
            qp, kp, vp = _na_qkv(hp, w_qkv[j], qk_g[j])
            op = _merge_heads(_ctx_attention(qp, kp, vp)) @ w_out_na[j]
            new_k.append(kp)
            new_v.append(vp)
            qs, ks_, vs = _na_qkv(hs, w_qkv[j], qk_g[j])
            os_ = _merge_heads(_na_latent(qs, ks_, vs, cache_k[:, j], cache_v[:, j], rpb[j])) @ w_out_na[j]
        yp = yp + mp[:, 2][:, None, :] * op
        ys = ys + ms[:, 2][:, None, :] * os_
        hp = _mod_norm(yp, norm_g[l, 1], mp[:, 3], mp[:, 4])
        hs = _mod_norm(ys, norm_g[l, 1], ms[:, 3], ms[:, 4])
        yp = yp + mp[:, 5][:, None, :] * _peer(hp, peer_wq[l], peer_keys[l], peer_u[l], peer_v[l])
        ys = ys + ms[:, 5][:, None, :] * _peer(hs, peer_wq[l], peer_keys[l], peer_u[l], peer_v[l])
    new_cache_k = jnp.stack(new_k, axis=1)
    new_cache_v = jnp.stack(new_v, axis=1)
    new_state_fwd = jnp.stack(new_sf, axis=1)
    new_state_bwd = jnp.stack(new_sb, axis=1)
    return (yp, ys, new_cache_k, new_cache_v, new_state_fwd, new_state_bwd)
```

```python
import functools

import numpy as np
import jax
import jax.numpy as jnp
from jax import lax
from jax.experimental import pallas as pl
from jax.experimental.pallas import tpu as pltpu

F32 = jnp.float32
BF16 = jnp.bfloat16

GRID_W = 64
A_HEADS = 4
A_DK = 128
A_DV = 128
CONV_W = 31
GLA_CHUNK = 16
C_HEADS = 16
C_HD = 64
WIN_H = 8
WIN_W = 16
PEER_HEADS = 8
N_KEYS = 128
PEER_TOPK = 16
N_MOD = 6
EPS = 1e-6

LANES = 128
TOK_BLK = 256
CONV_HALO = 16
VMEM_LIMIT = 56 * 1024 * 1024


def _cparams(sem):
    return pltpu.CompilerParams(dimension_semantics=sem, vmem_limit_bytes=VMEM_LIMIT)


def _bdot(a, b):
    return jnp.dot(a.astype(BF16), b.astype(BF16), preferred_element_type=F32)


def _bdot_nt(a, b):
    return lax.dot_general(a.astype(BF16), b.astype(BF16), (((1,), (1,)), ((), ())), preferred_element_type=F32)


def _split2(a):
    hi = a.astype(BF16)
    lo = (a - hi.astype(F32)).astype(BF16)
    return hi, lo


def _split3(a):
    hi = a.astype(BF16)
    r = a - hi.astype(F32)
    mid = r.astype(BF16)
    lo = (r - mid.astype(F32)).astype(BF16)
    return hi, mid, lo


def _dot3(a, b):
    ah, al = _split2(a)
    bh, bl = _split2(b)
    d = lambda x, y: jnp.dot(x, y, preferred_element_type=F32)
    return d(ah, bh) + (d(ah, bl) + d(al, bh))


def _dot3_nt(a, b):
    ah, al = _split2(a)
    bh, bl = _split2(b)
    d = lambda x, y: lax.dot_general(x, y, (((1,), (1,)), ((), ())), preferred_element_type=F32)
    return d(ah, bh) + (d(ah, bl) + d(al, bh))


def _exact_left(mask_bf16, x):
    h, m, l = _split3(x)
    d = lambda y: jnp.dot(mask_bf16, y, preferred_element_type=F32)
    return d(h) + (d(m) + d(l))


def _mod_row(n_prompt, dec_seq, tm):
    def row(i):
        tok = i * tm
        return jnp.where(tok < n_prompt, 0, 1 + (tok - n_prompt) // dec_seq)
    return row


def _modnorm(x, g, shift, scale):
    ms = jnp.mean(x * x, axis=-1, keepdims=True)
    return x * lax.rsqrt(ms + EPS) * g * (1.0 + scale) + shift


def _mod_kernel(c_ref, w_ref, b_ref, o_ref):
    s = jax.nn.silu(c_ref[...])
    o_ref[0] = _dot3(s, w_ref[0]) + b_ref[0]


def _modulation(cond8, w_mod, b_mod):
    depth, d, n = w_mod.shape
    tn = 1536
    return pl.pallas_call(
        _mod_kernel,
        grid=(depth, n // tn),
        in_specs=[pl.BlockSpec((8, d), lambda l, j: (0, 0)),
                  pl.BlockSpec((1, d, tn), lambda l, j: (l, 0, j)),
                  pl.BlockSpec((1, 1, tn), lambda l, j: (l, 0, j))],
        out_specs=pl.BlockSpec((1, 8, tn), lambda l, j: (l, 0, j)),
        out_shape=jax.ShapeDtypeStruct((depth, 8, n), F32),
        compiler_params=_cparams(("arbitrary", "arbitrary")),
        name="modulation",
    )(cond8, w_mod, b_mod.reshape(depth, 1, n))


def _normlin_kernel(x_ref, g_ref, mod_ref, w_ref, *rest, shift_idx, scale_idx, precise, qk_tiles):
    if qk_tiles:
        bd_ref, qg_ref, o_ref, h_ref = rest
    else:
        o_ref, h_ref = rest
    j = pl.program_id(1)

    @pl.when(j == 0)
    def _():
        m = mod_ref[0]
        h = _modnorm(x_ref[...], g_ref[...], m[shift_idx:shift_idx + 1], m[scale_idx:scale_idx + 1])
        if precise:
            hi, lo = _split2(h)
            h_ref[0] = hi
            h_ref[1] = lo
        else:
            h_ref[0] = h.astype(BF16)

    w = w_ref[...]
    if precise:
        wh, wl = _split2(w)
        d = lambda a, b: jnp.dot(a, b, preferred_element_type=F32)
        acc = d(h_ref[0], wh) + (d(h_ref[0], wl) + d(h_ref[1], wh))
    else:
        acc = jnp.dot(h_ref[0], w.astype(BF16), preferred_element_type=F32)

    if qk_tiles:
        @pl.when(j < qk_tiles)
        def _():
            sh, sl = _split2(acc * acc)
            bd = bd_ref[...]
            ms = jnp.dot(sh, bd, preferred_element_type=F32) + jnp.dot(sl, bd, preferred_element_type=F32)
            o_ref[...] = acc * lax.rsqrt(ms + EPS) * qg_ref[...]

        @pl.when(j >= qk_tiles)
        def _():
            o_ref[...] = acc
    else:
        o_ref[...] = acc


def _normlin(x, g, mod, w, row_fn, tm, tn, shift_idx, scale_idx, precise=False, qk=None):
    ntok, d = x.shape
    n = w.shape[1]
    in_specs = [pl.BlockSpec((tm, d), lambda i, j: (i, 0)),
                pl.BlockSpec((1, d), lambda i, j: (0, 0)),
                pl.BlockSpec((1, N_MOD, d), lambda i, j: (row_fn(i), 0, 0)),
                pl.BlockSpec((d, tn), lambda i, j: (0, j))]
    args = [x, g.reshape(1, d), mod, w]
    qk_tiles = 0
    if qk is not None:
        bd, qg, n_qk = qk
        qk_tiles = n_qk // tn
        in_specs += [pl.BlockSpec((tn, tn), lambda i, j: (0, 0)),
                     pl.BlockSpec((1, tn), lambda i, j: (0, jnp.minimum(j, qk_tiles - 1)))]
        args += [bd, qg]
    return pl.pallas_call(
        functools.partial(_normlin_kernel, shift_idx=shift_idx, scale_idx=scale_idx, precise=precise,
                          qk_tiles=qk_tiles),
        grid=(ntok // tm, n // tn),
        in_specs=in_specs,
        out_specs=pl.BlockSpec((tm, tn), lambda i, j: (i, j)),
        out_shape=jax.ShapeDtypeStruct((ntok, n), F32),
        scratch_shapes=[pltpu.VMEM((2 if precise else 1, tm, d), BF16)],
        compiler_params=_cparams(("arbitrary", "arbitrary")),
        name="normlin",
    )(*args)


def _gla_kernel(tbl_ref, zq_ref, zf_ref, zv_ref, lb_ref, s0_ref, o_ref, sfin_ref, st_ref, *, reverse, layer_j):
    step = pl.program_id(1)
    tb = zq_ref.shape[0]
    nchunk = tb // GLA_CHUNK

    @pl.when(tbl_ref[2, step] == 1)
    def _():
        st_ref[...] = s0_ref[0, 0].T

    lbx = lb_ref[0]
    e = jnp.exp(lbx - jnp.max(lbx, axis=0, keepdims=True))
    p = e / jnp.sum(e, axis=0, keepdims=True)
    lb = jnp.zeros((1, A_DK), F32)
    for i in range(1, layer_j + 1):
        lb = lb + p[i:i + 1]

    z = zf_ref[...]
    log_sig = jnp.minimum(z, 0.0) - jnp.log1p(jnp.exp(-jnp.abs(z)))
    a_ = jnp.log1p(-lb) + log_sig
    b_ = jnp.log(lb)
    logf = jnp.maximum(a_, b_) + jnp.log1p(jnp.exp(-jnp.abs(a_ - b_)))
    k = (1.0 - lb) * jax.nn.sigmoid(-z)
    q = zq_ref[...] * (A_DK ** -0.5)
    v = zv_ref[...]

    row = lax.broadcasted_iota(jnp.int32, (tb, tb), 0)
    col = lax.broadcasted_iota(jnp.int32, (tb, tb), 1)
    same = (row // GLA_CHUNK) == (col // GLA_CHUNK)
    if reverse:
        incl = same & (col >= row)
        rest = same & (col < row)
    else:
        incl = same & (col <= row)
        rest = same & (col > row)
    incl_m = jnp.where(incl, 1.0, 0.0).astype(BF16)
    rest_m = jnp.where(rest, 1.0, 0.0).astype(BF16)
    bc = _exact_left(incl_m, logf)
    brem = _exact_left(rest_m, logf)
    qd = q * jnp.exp(bc)
    kinv = k * jnp.exp(-bc)
    att = jnp.where(incl, _bdot_nt(qd, kinv), 0.0)
    o_intra = _bdot(att, v)
    klast = k * jnp.exp(brem)
    vt = v.T
    ridx = lax.broadcasted_iota(jnp.int32, (tb, A_DK), 0) // GLA_CHUNK

    st = st_ref[...]
    outs = [None] * nchunk
    order = range(nchunk - 1, -1, -1) if reverse else range(nchunk)
    for c in order:
        lo = c * GLA_CHUNK
        outs[c] = _bdot_nt(qd[lo:lo + GLA_CHUNK], st)
        last = lo if reverse else lo + GLA_CHUNK - 1
        a_c = jnp.exp(bc[last:last + 1])
        st = a_c * st + _bdot(vt, jnp.where(ridx == c, klast, 0.0))
    st_ref[...] = st
    o_ref[...] = o_intra + jnp.concatenate(outs, axis=0)
    sfin_ref[0, 0] = st.T


def _gla(z, hg_lb_t, s0_all, tbl, *, reverse, layer_j, qcol, fcol, vcol):
    ntok = z.shape[0]
    nseq = s0_all.shape[0]
    nsteps = tbl.shape[1]
    tb = TOK_BLK
    n_rec = hg_lb_t.shape[1]
    d = 1 if reverse else 0
    grid_spec = pltpu.PrefetchScalarGridSpec(
        num_scalar_prefetch=1,
        grid=(A_HEADS, nsteps),
        in_specs=[pl.BlockSpec((tb, A_DK), lambda h, s, t: (t[0, s], qcol + h)),
                  pl.BlockSpec((tb, A_DK), lambda h, s, t: (t[0, s], fcol + h)),
                  pl.BlockSpec((tb, A_DV), lambda h, s, t: (t[0, s], vcol + h)),
                  pl.BlockSpec((1, n_rec, A_DK), lambda h, s, t: (d, 0, h)),
                  pl.BlockSpec((1, 1, A_DK, A_DV), lambda h, s, t: (t[1, s], h, 0, 0))],
        out_specs=[pl.BlockSpec((tb, A_DV), lambda h, s, t: (t[0, s], h)),
                   pl.BlockSpec((1, 1, A_DK, A_DV), lambda h, s, t: (t[1, s], h, 0, 0))],
        scratch_shapes=[pltpu.VMEM((A_DV, A_DK), F32)])
    return pl.pallas_call(
        functools.partial(_gla_kernel, reverse=reverse, layer_j=layer_j),
        grid_spec=grid_spec,
        out_shape=[jax.ShapeDtypeStruct((ntok, A_HEADS * A_DV), F32),
                   jax.ShapeDtypeStruct((nseq, A_HEADS, A_DK, A_DV), F32)],
        compiler_params=_cparams(("arbitrary", "arbitrary")),
        name="gla_bwd" if reverse else "gla_fwd",
    )(tbl, z, z, z, hg_lb_t, s0_all)


def _conv_kernel(tbl_ref, ua_ref, ug_ref, pa_ref, pg_ref, na_ref, ng_ref, w_ref, b_ref, lg_ref, lbias_ref, o_ref,
                 buf_ref):
    i = pl.program_id(0)
    tb = ua_ref.shape[0]
    hl = CONV_HALO
    prev_ok = tbl_ref[0, i] == 1
    next_ok = tbl_ref[1, i] == 1
    buf_ref[hl:hl + tb, :] = ua_ref[...] * jax.nn.sigmoid(ug_ref[...])
    buf_ref[0:hl, :] = jnp.where(prev_ok, pa_ref[...] * jax.nn.sigmoid(pg_ref[...]), 0.0)
    buf_ref[hl + tb:hl + tb + hl, :] = jnp.where(next_ok, na_ref[...] * jax.nn.sigmoid(ng_ref[...]), 0.0)
    w = w_ref[...]
    acc = jnp.zeros(o_ref.shape, F32) + b_ref[...]
    base = hl - CONV_W // 2
    for t in range(CONV_W):
        acc = acc + buf_ref[base + t:base + t + tb, :] * w[t:t + 1]
    mu = jnp.mean(acc, axis=-1, keepdims=True)
    xc = acc - mu
    var = jnp.mean(xc * xc, axis=-1, keepdims=True)
    y = xc * lax.rsqrt(var + EPS) * lg_ref[...] + lbias_ref[...]
    o_ref[...] = jax.nn.silu(y)


def _conv(z, tbl, cv_w, cv_b, ln_g, ln_b, *, acol, gcol):
    ntok = z.shape[0]
    ch = cv_w.shape[1]
    tb = TOK_BLK
    hl = CONV_HALO
    r = tb // hl
    last = ntok // hl - 1
    prev = lambda i, t: jnp.maximum(i * r - 1, 0)
    nxt = lambda i, t: jnp.minimum((i + 1) * r, last)
    grid_spec = pltpu.PrefetchScalarGridSpec(
        num_scalar_prefetch=1,
        grid=(ntok // tb,),
        in_specs=[pl.BlockSpec((tb, ch), lambda i, t: (i, acol)),
                  pl.BlockSpec((tb, ch), lambda i, t: (i, gcol)),
                  pl.BlockSpec((hl, ch), lambda i, t: (prev(i, t), acol)),
                  pl.BlockSpec((hl, ch), lambda i, t: (prev(i, t), gcol)),
                  pl.BlockSpec((hl, ch), lambda i, t: (nxt(i, t), acol)),
                  pl.BlockSpec((hl, ch), lambda i, t: (nxt(i, t), gcol)),
                  pl.BlockSpec((CONV_W, ch), lambda i, t: (0, 0)),
                  pl.BlockSpec((1, ch), lambda i, t: (0, 0)),
                  pl.BlockSpec((1, ch), lambda i, t: (0, 0)),
                  pl.BlockSpec((1, ch), lambda i, t: (0, 0))],
        out_specs=pl.BlockSpec((tb, ch), lambda i, t: (i, 0)),
        scratch_shapes=[pltpu.VMEM((tb + 2 * hl, ch), F32)])
    return pl.pallas_call(
        _conv_kernel,
        grid_spec=grid_spec,
        out_shape=jax.ShapeDtypeStruct((ntok, ch), F32),
        compiler_params=_cparams(("arbitrary",)),
        name="conv_module",
    )(tbl, z, z, z, z, z, z, cv_w, cv_b.reshape(1, ch), ln_g.reshape(1, ch), ln_b.reshape(1, ch))


def _even_out_kernel(x_ref, mod_ref, of_ref, ob_ref, zg_ref, oc_ref, gn_ref, w_ref, y_ref, *, gate_idx):
    o = of_ref[...] + ob_ref[...]
    gate = jax.nn.silu(zg_ref[...])
    gn = gn_ref[...]
    parts = []
    for h in range(A_HEADS):
        oh = o[:, h * A_DV:(h + 1) * A_DV]
        ms = jnp.mean(oh * oh, axis=-1, keepdims=True)
        parts.append(oh * lax.rsqrt(ms + EPS) * gn * gate[:, h * A_DV:(h + 1) * A_DV])
    o_a = jnp.concatenate(parts, axis=-1)
    na = o_a.shape[1]
    acc = _bdot(o_a, w_ref[0:na, :]) + _bdot(oc_ref[...], w_ref[na:, :])
    g = mod_ref[0][gate_idx:gate_idx + 1]
    y_ref[...] = x_ref[...] + g * acc


def _even_out(x, mod, o_f, o_b, z, o_c, gn_g, w_out, row_fn, tm, gate_idx, gcol):
    ntok, d = x.shape
    na = o_f.shape[1]
    nc = o_c.shape[1]
    return pl.pallas_call(
        functools.partial(_even_out_kernel, gate_idx=gate_idx),
        grid=(ntok // tm,),
        in_specs=[pl.BlockSpec((tm, d), lambda i: (i, 0)),
                  pl.BlockSpec((1, N_MOD, d), lambda i: (row_fn(i), 0, 0)),
                  pl.BlockSpec((tm, na), lambda i: (i, 0)),
                  pl.BlockSpec((tm, na), lambda i: (i, 0)),
                  pl.BlockSpec((tm, na), lambda i: (i, gcol)),
                  pl.BlockSpec((tm, nc), lambda i: (i, 0)),
                  pl.BlockSpec((1, A_DV), lambda i: (0, 0)),
                  pl.BlockSpec((na + nc, d), lambda i: (0, 0))],
        out_specs=pl.BlockSpec((tm, d), lambda i: (i, 0)),
        out_shape=jax.ShapeDtypeStruct((ntok, d), F32),
        compiler_params=_cparams(("arbitrary",)),
        name="even_out",
    )(x, mod, o_f, o_b, z, o_c, gn_g.reshape(1, A_DV), w_out)


def _lin_out_kernel(x_ref, mod_ref, o_ref, w_ref, y_ref, *, gate_idx):
    g = mod_ref[0][gate_idx:gate_idx + 1]
    y_ref[...] = x_ref[...] + g * _bdot(o_ref[...], w_ref[...])


def _lin_out(x, mod, o, w, row_fn, tm, gate_idx):
    ntok, d = x.shape
    kdim = o.shape[1]
    return pl.pallas_call(
        functools.partial(_lin_out_kernel, gate_idx=gate_idx),
        grid=(ntok // tm,),
        in_specs=[pl.BlockSpec((tm, d), lambda i: (i, 0)),
                  pl.BlockSpec((1, N_MOD, d), lambda i: (row_fn(i), 0, 0)),
                  pl.BlockSpec((tm, kdim), lambda i: (i, 0)),
                  pl.BlockSpec((kdim, d), lambda i: (0, 0))],
        out_specs=pl.BlockSpec((tm, d), lambda i: (i, 0)),
        out_shape=jax.ShapeDtypeStruct((ntok, d), F32),
        compiler_params=_cparams(("arbitrary",)),
        name="lin_out",
    )(x, mod, o, w)


def _ctx_attn_kernel(q_ref, k_ref, v_ref, o_ref):
    scale = C_HD ** -0.5
    for h in range(C_HEADS):
        sl = slice(h * C_HD, (h + 1) * C_HD)
        s = _bdot_nt(q_ref[:, sl], k_ref[:, sl]) * scale
        e = jnp.exp(s - jnp.max(s, axis=-1, keepdims=True))
        p = e / jnp.sum(e, axis=-1, keepdims=True)
        o_ref[:, sl] = _bdot(p, v_ref[:, sl])


def _ctx_attn(zqkv, n_prompt, seq):
    ntok = zqkv.shape[0]
    w = C_HEADS * C_HD
    return pl.pallas_call(
        _ctx_attn_kernel,
        grid=(n_prompt // seq,),
        in_specs=[pl.BlockSpec((seq, w), lambda b: (b, 0)),
                  pl.BlockSpec((seq, w), lambda b: (b, 1)),
                  pl.BlockSpec((seq, w), lambda b: (b, 2))],
        out_specs=pl.BlockSpec((seq, w), lambda b: (b, 0)),
        out_shape=jax.ShapeDtypeStruct((ntok, w), F32),
        compiler_params=_cparams(("arbitrary",)),
        name="ctx_attn",
    )(zqkv, zqkv, zqkv)


def _na_kernel(o_in_ref, q_ref, k_ref, v_ref, kc_ref, vc_ref, bias_ref, o_ref, *, rows):
    del o_in_ref
    scale = C_HD ** -0.5
    kh = min(WIN_H, rows)
    nloc = kh * GRID_W

    def body(rq, carry):
        r0 = jnp.clip(rq - kh // 2, 0, rows - kh)
        off = r0 - rq + (WIN_H - 1)
        qrow = q_ref[pl.ds(pl.multiple_of(rq * GRID_W, GRID_W), GRID_W), :]
        kwin = k_ref[pl.ds(pl.multiple_of(r0 * GRID_W, GRID_W), nloc), :]
        vwin = v_ref[pl.ds(pl.multiple_of(r0 * GRID_W, GRID_W), nloc), :]
        outs = []
        for hh in range(LANES // C_HD):
            sl = slice(hh * C_HD, (hh + 1) * C_HD)
            qh = qrow[:, sl]
            s_loc = _bdot_nt(qh, kwin[:, sl]) * scale + bias_ref[hh, off]
            s_ctx = _bdot_nt(qh, kc_ref[0, 0, hh]) * scale
            m = jnp.maximum(jnp.max(s_loc, axis=-1, keepdims=True), jnp.max(s_ctx, axis=-1, keepdims=True))
            e_loc = jnp.exp(s_loc - m)
            e_ctx = jnp.exp(s_ctx - m)
            den = jnp.sum(e_loc, axis=-1, keepdims=True) + jnp.sum(e_ctx, axis=-1, keepdims=True)
            inv = 1.0 / den
            outs.append(_bdot(e_loc * inv, vwin[:, sl]) + _bdot(e_ctx * inv, vc_ref[0, 0, hh]))
        o_ref[pl.ds(pl.multiple_of(rq * GRID_W, GRID_W), GRID_W), :] = jnp.concatenate(outs, axis=-1)
        return carry

    lax.fori_loop(0, rows, body, 0)


def _na_attn(o_prompt, zqkv, cache_k, cache_v, bias8, layer_j, n_prompt, dec_seq):
    ntok = zqkv.shape[0]
    nb = (ntok - n_prompt) // dec_seq
    rows = dec_seq // GRID_W
    boff = n_prompt // dec_seq
    hp = LANES // C_HD
    npair = C_HEADS // hp
    past = cache_k.shape[3]
    return pl.pallas_call(
        functools.partial(_na_kernel, rows=rows),
        grid=(npair, nb),
        in_specs=[pl.BlockSpec(memory_space=pl.ANY),
                  pl.BlockSpec((dec_seq, LANES), lambda p, b: (boff + b, p)),
                  pl.BlockSpec((dec_seq, LANES), lambda p, b: (boff + b, npair + p)),
                  pl.BlockSpec((dec_seq, LANES), lambda p, b: (boff + b, 2 * npair + p)),
                  pl.BlockSpec((1, 1, hp, past, C_HD), lambda p, b: (b, layer_j, p, 0, 0)),
                  pl.BlockSpec((1, 1, hp, past, C_HD), lambda p, b: (b, layer_j, p, 0, 0)),
                  pl.BlockSpec((hp, WIN_H, GRID_W, min(WIN_H, rows) * GRID_W), lambda p, b: (p, 0, 0, 0))],
        out_specs=pl.BlockSpec((dec_seq, LANES), lambda p, b: (boff + b, p)),
        out_shape=jax.ShapeDtypeStruct(o_prompt.shape, F32),
        input_output_aliases={0: 0},
        compiler_params=_cparams(("arbitrary", "arbitrary")),
        name="na_attn",
    )(o_prompt, zqkv, zqkv, zqkv, cache_k, cache_v, bias8)


def _na_bias(rpb_l, rows):
    kh = min(WIN_H, rows)
    cq = jnp.arange(GRID_W)
    c0 = jnp.clip(cq - WIN_W // 2, 0, GRID_W - WIN_W)
    in_win = (cq[None, :] >= c0[:, None]) & (cq[None, :] < c0[:, None] + WIN_W)
    dc = jnp.clip(cq[None, :] - cq[:, None] + (WIN_W - 1), 0, 2 * WIN_W - 2)
    dr = jnp.clip(jnp.arange(WIN_H)[:, None] + jnp.arange(kh)[None, :], 0, 2 * WIN_H - 2)
    b = rpb_l[:, dr[:, None, :, None], dc[None, :, None, :]].astype(F32)
    b = jnp.where(in_win[None, None, :, None, :], b, -jnp.inf)
    return b.reshape(rpb_l.shape[0], WIN_H, GRID_W, kh * GRID_W)


def _top_desc(x, n):
    vals = []
    for _ in range(n):
        m = jnp.max(x, axis=0, keepdims=True)
        vals.append(m)
        x = jnp.where(x == m, -jnp.inf, x)
    return vals


def _peer_gate_kernel(hq_ref, keys_ref, a_ref, b_ref, tau_ref):
    kq = PEER_TOPK
    for h in range(PEER_HEADS):
        s = []
        tops = []
        for p in range(2):
            c0 = (h * 2 + p) * N_KEYS
            sp = _dot3_nt(keys_ref[h, p], hq_ref[:, c0:c0 + N_KEYS])
            s.append(sp)
            tops.append(_top_desc(sp, kq + 1))
        cand = jnp.concatenate([tops[0][r] + jnp.concatenate(tops[1][:kq], axis=0) for r in range(kq)], axis=0)
        best = _top_desc(cand, kq + 1)
        mx = best[0]
        z = jnp.zeros_like(mx)
        for r in range(kq):
            z = z + jnp.exp(best[r] - mx)
        nxt = jnp.maximum(best[kq], jnp.maximum(tops[0][kq] + tops[1][0], tops[0][0] + tops[1][kq]))
        thr = 0.5 * (best[kq - 1] + nxt)
        inv_z = 1.0 / z
        a_ref[h] = jnp.exp(s[0] - tops[0][0]) * inv_z
        b_ref[h] = jnp.exp(s[1] - tops[1][0])
        tau_ref[h:h + 1, :] = jnp.exp(thr - mx) * inv_z


def _peer_gates(hq, keys, tmk):
    ntok = hq.shape[0]
    return pl.pallas_call(
        _peer_gate_kernel,
        grid=(ntok // tmk,),
        in_specs=[pl.BlockSpec((tmk, hq.shape[1]), lambda i: (i, 0)),
                  pl.BlockSpec(keys.shape, lambda i: (0, 0, 0, 0))],
        out_specs=[pl.BlockSpec((PEER_HEADS, N_KEYS, tmk), lambda i: (0, 0, i)),
                   pl.BlockSpec((PEER_HEADS, N_KEYS, tmk), lambda i: (0, 0, i)),
                   pl.BlockSpec((PEER_HEADS, tmk), lambda i: (0, i))],
        out_shape=[jax.ShapeDtypeStruct((PEER_HEADS, N_KEYS, ntok), F32),
                   jax.ShapeDtypeStruct((PEER_HEADS, N_KEYS, ntok), F32),
                   jax.ShapeDtypeStruct((PEER_HEADS, ntok), F32)],
        compiler_params=_cparams(("arbitrary",)),
        name="peer_gates",
    )(hq, keys)


def _peer_mix_kernel(x_ref, g_ref, mod_ref, u_ref, vt_ref, a_ref, b_ref, tau_ref, y_ref, h_ref, ht_ref, acc_ref, *,
                     shift_idx, scale_idx, gate_idx):
    e = pl.program_id(1)
    te = u_ref.shape[0]
    nsub = te // N_KEYS

    @pl.when(e == 0)
    def _():
        m = mod_ref[0]
        h = _modnorm(x_ref[...], g_ref[...], m[shift_idx:shift_idx + 1], m[scale_idx:scale_idx + 1])
        h_ref[...] = h.astype(BF16)
        acc_ref[...] = jnp.zeros_like(acc_ref)

    act = lax.dot_general(u_ref[...], h_ref[...], (((1,), (1,)), ((), ())), preferred_element_type=F32)
    for ii in range(nsub):
        i = e * nsub + ii
        gsum = None
        for h in range(PEER_HEADS):
            w = a_ref[h, pl.ds(i, 1), :] * b_ref[h]
            term = jnp.where(w >= tau_ref[h:h + 1, :], w, 0.0)
            gsum = term if gsum is None else gsum + term
        blk = jax.nn.gelu(act[ii * N_KEYS:(ii + 1) * N_KEYS]) * gsum
        ht_ref[ii * N_KEYS:(ii + 1) * N_KEYS, :] = blk.astype(BF16)
    acc_ref[...] += jnp.dot(vt_ref[...], ht_ref[...], preferred_element_type=F32)

    @pl.when(e == pl.num_programs(1) - 1)
    def _():
        g = mod_ref[0][gate_idx:gate_idx + 1]
        y_ref[...] = x_ref[...] + g * acc_ref[...].T


def _peer_mix(x, g, mod, u_bf, vt_bf, a, b, tau, row_fn, tm, te, shift_idx, scale_idx, gate_idx):
    ntok, d = x.shape
    nexp = u_bf.shape[0]
    return pl.pallas_call(
        functools.partial(_peer_mix_kernel, shift_idx=shift_idx, scale_idx=scale_idx, gate_idx=gate_idx),
        grid=(ntok // tm, nexp // te),
        in_specs=[pl.BlockSpec((tm, d), lambda t, e: (t, 0)),
                  pl.BlockSpec((1, d), lambda t, e: (0, 0)),
                  pl.BlockSpec((1, N_MOD, d), lambda t, e: (row_fn(t), 0, 0)),
                  pl.BlockSpec((te, d), lambda t, e: (e, 0)),
                  pl.BlockSpec((d, te), lambda t, e: (0, e)),
                  pl.BlockSpec((PEER_HEADS, N_KEYS, tm), lambda t, e: (0, 0, t)),
                  pl.BlockSpec((PEER_HEADS, N_KEYS, tm), lambda t, e: (0, 0, t)),
                  pl.BlockSpec((PEER_HEADS, tm), lambda t, e: (0, t))],
        out_specs=pl.BlockSpec((tm, d), lambda t, e: (t, 0)),
        out_shape=jax.ShapeDtypeStruct((ntok, d), F32),
        scratch_shapes=[pltpu.VMEM((tm, d), BF16),
                        pltpu.VMEM((te, tm), BF16),
                        pltpu.VMEM((d, tm), F32)],
        compiler_params=_cparams(("arbitrary", "arbitrary")),
        name="peer_mix",
    )(x, g.reshape(1, d), mod, u_bf, vt_bf, a, b, tau)


def _seq_tables(n_prompt, seq, n_dec, dec_seq, tb):
    seq_of, first, last = [], [], []
    sid = 0
    for count, length in ((n_prompt // seq, seq), (n_dec, dec_seq)):
        nb = length // tb
        for _ in range(count):
            for b in range(nb):
                seq_of.append(sid)
                first.append(int(b == 0))
                last.append(int(b == nb - 1))
            sid += 1
    nblk = len(seq_of)
    fwd = np.array([list(range(nblk)), seq_of, first], np.int32)
    order = list(range(nblk - 1, -1, -1))
    bwd = np.array([order, [seq_of[i] for i in order], [last[i] for i in order]], np.int32)
    halo = np.array([[1 - f for f in first], [1 - l for l in last]], np.int32)
    return fwd, bwd, halo


def kernel(x_prompt, x_sample, cache_k, cache_v, state_fwd, state_bwd, c, c_ctx, norm_g, w_mod, b_mod, w_in_ab, hg_lb,
           hg_norm_g, cv_w, cv_b, cv_ln_g, cv_ln_b, w_out_ab, w_qkv, qk_g, rpb, w_out_na, peer_wq, peer_keys, peer_u,
           peer_v):
    bp, seq, d = x_prompt.shape
    nd, dec_seq, _ = x_sample.shape
    depth = w_mod.shape[0]
    n_prompt = bp * seq
    ntok = n_prompt + nd * dec_seq
    assert n_prompt % dec_seq == 0 and seq % TOK_BLK == 0 and dec_seq % TOK_BLK == 0
    rows = dec_seq // GRID_W
    n_qk = A_HEADS * A_DK
    n_v = A_HEADS * A_DV
    n_b = d - n_v

    tile = lambda limit: max(t for t in (128, 256, 512, 1024) if t <= limit and n_prompt % t == 0 and dec_seq % t == 0)
    tm_lin = tile(1024)
    tm_out = tile(512)
    tm_peer = tile(512)
    te_peer = 512
    tmk = 256

    x = jnp.concatenate([x_prompt.reshape(n_prompt, d), x_sample.reshape(nd * dec_seq, d)], axis=0)
    cond8 = jnp.zeros((8, d), F32).at[0].set(c_ctx).at[1:1 + nd].set(c)
    mods = _modulation(cond8, w_mod, b_mod).reshape(depth, 8, N_MOD, d)

    fwd_tbl, bwd_tbl, halo_tbl = _seq_tables(n_prompt, seq, nd, dec_seq, TOK_BLK)
    fwd_tbl, bwd_tbl, halo_tbl = jnp.asarray(fwd_tbl), jnp.asarray(bwd_tbl), jnp.asarray(halo_tbl)
    hg_lb_t = jnp.transpose(hg_lb, (1, 0, 2))
    zeros_state = jnp.zeros((bp, A_HEADS, A_DK, A_DV), F32)

    row = lambda tm: _mod_row(n_prompt, dec_seq, tm)
    tn_qkv = 512
    bd = (jnp.arange(tn_qkv)[:, None] // C_HD == jnp.arange(tn_qkv)[None, :] // C_HD).astype(BF16) * (1.0 / C_HD)

    new_k, new_v, new_sf, new_sb = [], [], [], []
    for l in range(depth):
        j = l // 2
        mod = mods[l]
        if l % 2 == 0:
            z = _normlin(x, norm_g[l, 0], mod, w_in_ab[j], row(tm_lin), tm_lin, 512, 0, 1)
            blk = lambda col: col // A_DK
            s0f = jnp.concatenate([zeros_state, state_fwd[:, j]], axis=0)
            s0b = jnp.concatenate([zeros_state, state_bwd[:, j]], axis=0)
            o_f, sf = _gla(z, hg_lb_t, s0f, fwd_tbl, reverse=False, layer_j=j,
                           qcol=0, fcol=blk(n_qk), vcol=blk(3 * n_qk))
            o_b, sb = _gla(z, hg_lb_t, s0b, bwd_tbl, reverse=True, layer_j=j,
                           qcol=0, fcol=blk(2 * n_qk), vcol=blk(3 * n_qk))
            new_sf.append(sf[:bp])
            new_sb.append(sb[:bp])
            cbase = 3 * n_qk + 2 * n_v
            o_c = _conv(z, halo_tbl, cv_w[j], cv_b[j], cv_ln_g[j], cv_ln_b[j],
                        acol=cbase // n_b, gcol=(cbase + n_b) // n_b)
            x = _even_out(x, mod, o_f, o_b, z, o_c, hg_norm_g[j], w_out_ab[j], row(tm_out), tm_out, 2,
                          (3 * n_qk + n_v) // n_v)
        else:
            wq = C_HEADS * C_HD
            qg = jnp.concatenate([jnp.tile(qk_g[j, 0], C_HEADS), jnp.tile(qk_g[j, 1], C_HEADS)]).reshape(1, 2 * wq)
            z = _normlin(x, norm_g[l, 0], mod, w_qkv[j], row(tm_lin), tm_lin, tn_qkv, 0, 1, qk=(bd, qg, 2 * wq))
            kv = z[:n_prompt].reshape(bp, seq, 3, C_HEADS, C_HD)
            new_k.append(jnp.transpose(kv[:, :, 1], (0, 2, 1, 3)))
            new_v.append(jnp.transpose(kv[:, :, 2], (0, 2, 1, 3)))
            o = _ctx_attn(z, n_prompt, seq)
            o = _na_attn(o, z, cache_k, cache_v, _na_bias(rpb[j], rows), j, n_prompt, dec_seq)
            x = _lin_out(x, mod, o, w_out_na[j], row(tm_out), tm_out, 2)
        hq = _normlin(x, norm_g[l, 1], mod, peer_wq[l], row(tm_lin), tm_lin, 512, 3, 4, precise=True)
        a, b, tau = _peer_gates(hq, peer_keys[l], tmk)
        x = _peer_mix(x, norm_g[l, 1], mod, peer_u[l].astype(BF16), peer_v[l].T.astype(BF16), a, b, tau,
                      row(tm_peer), tm_peer, te_peer, 3, 4, 5)

    yp = x[:n_prompt].reshape(bp, seq, d)
    ys = x[n_prompt:].reshape(nd, dec_seq, d)
    return (yp, ys, jnp.stack(new_k, axis=1), jnp.stack(new_v, axis=1),
            jnp.stack(new_sf, axis=1), jnp.stack(new_sb, axis=1))
```

```python
import functools

import numpy as np
import jax
import jax.numpy as jnp
from jax import lax
from jax.experimental import pallas as pl
from jax.experimental.pallas import tpu as pltpu

F32 = jnp.float32
BF16 = jnp.bfloat16

GRID_W = 64
A_HEADS = 4
A_DK = 128
A_DV = 128
CONV_W = 31
GLA_CHUNK = 16
C_HEADS = 16
C_HD = 64
WIN_H = 8
WIN_W = 16
PEER_HEADS = 8
N_KEYS = 128
PEER_TOPK = 16
N_MOD = 6
EPS = 1e-6

LANES = 128
TOK_BLK = 256
CONV_HALO = 16
VMEM_LIMIT = 56 * 1024 * 1024


def _cparams(sem):
    return pltpu.CompilerParams(dimension_semantics=sem, vmem_limit_bytes=VMEM_LIMIT)


def _bdot(a, b):
    return jnp.dot(a.astype(BF16), b.astype(BF16), preferred_element_type=F32)


def _bdot_nt(a, b):
    return lax.dot_general(a.astype(BF16), b.astype(BF16), (((1,), (1,)), ((), ())), preferred_element_type=F32)


def _split2(a):
    hi = a.astype(BF16)
    lo = (a - hi.astype(F32)).astype(BF16)
    return hi, lo


def _split3(a):
    hi = a.astype(BF16)
    r = a - hi.astype(F32)
    mid = r.astype(BF16)
    lo = (r - mid.astype(F32)).astype(BF16)
    return hi, mid, lo


def _dot3(a, b):
    ah, al = _split2(a)
    bh, bl = _split2(b)
    d = lambda x, y: jnp.dot(x, y, preferred_element_type=F32)
    return d(ah, bh) + (d(ah, bl) + d(al, bh))


def _dot3_nt(a, b):
    ah, al = _split2(a)
    bh, bl = _split2(b)
    d = lambda x, y: lax.dot_general(x, y, (((1,), (1,)), ((), ())), preferred_element_type=F32)
    return d(ah, bh) + (d(ah, bl) + d(al, bh))


def _exact_left(mask_bf16, x):
    h, m, l = _split3(x)
    d = lambda y: jnp.dot(mask_bf16, y, preferred_element_type=F32)
    return d(h) + (d(m) + d(l))


def _mod_row(n_prompt, dec_seq, tm):
    def row(i):
        tok = i * tm
        return jnp.where(tok < n_prompt, 0, 1 + (tok - n_prompt) // dec_seq)
    return row


def _modnorm(x, g, shift, scale):
    ms = jnp.mean(x * x, axis=-1, keepdims=True)
    return x * lax.rsqrt(ms + EPS) * g * (1.0 + scale) + shift


def _mod_kernel(c_ref, w_ref, b_ref, o_ref):
    s = jax.nn.silu(c_ref[...])
    o_ref[0] = _dot3(s, w_ref[0]) + b_ref[0]


def _modulation(cond8, w_mod, b_mod):
    depth, d, n = w_mod.shape
    tn = 1536
    return pl.pallas_call(
        _mod_kernel,
        grid=(depth, n // tn),
        in_specs=[pl.BlockSpec((8, d), lambda l, j: (0, 0)),
                  pl.BlockSpec((1, d, tn), lambda l, j: (l, 0, j)),
                  pl.BlockSpec((1, 1, tn), lambda l, j: (l, 0, j))],
        out_specs=pl.BlockSpec((1, 8, tn), lambda l, j: (l, 0, j)),
        out_shape=jax.ShapeDtypeStruct((depth, 8, n), F32),
        compiler_params=_cparams(("arbitrary", "arbitrary")),
        name="modulation",
    )(cond8, w_mod, b_mod.reshape(depth, 1, n))


def _normlin_kernel(x_ref, g_ref, mod_ref, w_ref, *rest, shift_idx, scale_idx, precise, qk_tiles):
    if qk_tiles:
        bd_ref, qg_ref, o_ref, h_ref = rest
    else:
        o_ref, h_ref = rest
    j = pl.program_id(1)

    @pl.when(j == 0)
    def _():
        m = mod_ref[0]
        h = _modnorm(x_ref[...], g_ref[...], m[shift_idx:shift_idx + 1], m[scale_idx:scale_idx + 1])
        if precise:
            hi, lo = _split2(h)
            h_ref[0] = hi
            h_ref[1] = lo
        else:
            h_ref[0] = h.astype(BF16)

    w = w_ref[...]
    if precise:
        wh, wl = _split2(w)
        d = lambda a, b: jnp.dot(a, b, preferred_element_type=F32)
        acc = d(h_ref[0], wh) + (d(h_ref[0], wl) + d(h_ref[1], wh))
    else:
        acc = jnp.dot(h_ref[0], w.astype(BF16), preferred_element_type=F32)

    if qk_tiles:
        @pl.when(j < qk_tiles)
        def _():
            sh, sl = _split2(acc * acc)
            bd = bd_ref[...]
            ms = jnp.dot(sh, bd, preferred_element_type=F32) + jnp.dot(sl, bd, preferred_element_type=F32)
            o_ref[...] = acc * lax.rsqrt(ms + EPS) * qg_ref[...]

        @pl.when(j >= qk_tiles)
        def _():
            o_ref[...] = acc
    else:
        o_ref[...] = acc


def _normlin(x, g, mod, w, row_fn, tm, tn, shift_idx, scale_idx, precise=False, qk=None):
    ntok, d = x.shape
    n = w.shape[1]
    in_specs = [pl.BlockSpec((tm, d), lambda i, j: (i, 0)),
                pl.BlockSpec((1, d), lambda i, j: (0, 0)),
                pl.BlockSpec((1, N_MOD, d), lambda i, j: (row_fn(i), 0, 0)),
                pl.BlockSpec((d, tn), lambda i, j: (0, j))]
    args = [x, g.reshape(1, d), mod, w]
    qk_tiles = 0
    if qk is not None:
        bd, qg, n_qk = qk
        qk_tiles = n_qk // tn
        in_specs += [pl.BlockSpec((tn, tn), lambda i, j: (0, 0)),
                     pl.BlockSpec((1, tn), lambda i, j: (0, jnp.minimum(j, qk_tiles - 1)))]
        args += [bd, qg]
    return pl.pallas_call(
        functools.partial(_normlin_kernel, shift_idx=shift_idx, scale_idx=scale_idx, precise=precise,
                          qk_tiles=qk_tiles),
        grid=(ntok // tm, n // tn),
        in_specs=in_specs,
        out_specs=pl.BlockSpec((tm, tn), lambda i, j: (i, j)),
        out_shape=jax.ShapeDtypeStruct((ntok, n), F32),
        scratch_shapes=[pltpu.VMEM((2 if precise else 1, tm, d), BF16)],
        compiler_params=_cparams(("arbitrary", "arbitrary")),
        name="normlin",
    )(*args)


GLA_HEADS_PER_STEP = 2


def _gla_kernel(tbl_ref, zq_ref, zf_ref, zv_ref, lb_ref, s0_ref, o_ref, sfin_ref, st_ref, *, reverse, layer_j):
    step = pl.program_id(1)
    tb = zq_ref.shape[0]
    nchunk = tb // GLA_CHUNK
    nh = zq_ref.shape[1] // A_DK

    @pl.when(tbl_ref[2, step] == 1)
    def _():
        for hh in range(nh):
            st_ref[hh] = s0_ref[0, hh].T

    row = lax.broadcasted_iota(jnp.int32, (tb, tb), 0)
    col = lax.broadcasted_iota(jnp.int32, (tb, tb), 1)
    same = (row // GLA_CHUNK) == (col // GLA_CHUNK)
    if reverse:
        incl = same & (col >= row)
        rest = same & (col < row)
    else:
        incl = same & (col <= row)
        rest = same & (col > row)
    incl_m = jnp.where(incl, 1.0, 0.0).astype(BF16)
    rest_m = jnp.where(rest, 1.0, 0.0).astype(BF16)
    ridx = lax.broadcasted_iota(jnp.int32, (tb, A_DK), 0) // GLA_CHUNK

    lbx = lb_ref[0]
    e = jnp.exp(lbx - jnp.max(lbx, axis=0, keepdims=True))
    p = e / jnp.sum(e, axis=0, keepdims=True)
    lb_all = jnp.zeros((1, lbx.shape[1]), F32)
    for i in range(1, layer_j + 1):
        lb_all = lb_all + p[i:i + 1]

    for hh in range(nh):
        sl = slice(hh * A_DK, (hh + 1) * A_DK)
        lb = lb_all[:, sl]
        z = zf_ref[:, sl]
        log_sig = jnp.minimum(z, 0.0) - jnp.log1p(jnp.exp(-jnp.abs(z)))
        a_ = jnp.log1p(-lb) + log_sig
        b_ = jnp.log(lb)
        logf = jnp.maximum(a_, b_) + jnp.log1p(jnp.exp(-jnp.abs(a_ - b_)))
        k = (1.0 - lb) * jax.nn.sigmoid(-z)
        q = zq_ref[:, sl] * (A_DK ** -0.5)
        v = zv_ref[:, sl]

        bc = _exact_left(incl_m, logf)
        brem = _exact_left(rest_m, logf)
        qd = q * jnp.exp(bc)
        kinv = k * jnp.exp(-bc)
        att = jnp.where(incl, _bdot_nt(qd, kinv), 0.0)
        o_intra = _bdot(att, v)
        klast = k * jnp.exp(brem)
        vt = v.T

        st = st_ref[hh]
        outs = [None] * nchunk
        order = range(nchunk - 1, -1, -1) if reverse else range(nchunk)
        for c in order:
            lo = c * GLA_CHUNK
            outs[c] = _bdot_nt(qd[lo:lo + GLA_CHUNK], st)
            last = lo if reverse else lo + GLA_CHUNK - 1
            a_c = jnp.exp(bc[last:last + 1])
            st = a_c * st + _bdot(vt, jnp.where(ridx == c, klast, 0.0))
        st_ref[hh] = st
        o_ref[:, sl] = o_intra + jnp.concatenate(outs, axis=0)
        sfin_ref[0, hh] = st.T


def _gla(z, hg_lb_t, s0_all, tbl, *, reverse, layer_j, qcol, fcol, vcol):
    ntok = z.shape[0]
    nseq = s0_all.shape[0]
    nsteps = tbl.shape[1]
    tb = TOK_BLK
    n_rec = hg_lb_t.shape[1]
    d = 1 if reverse else 0
    nh = GLA_HEADS_PER_STEP
    w = nh * A_DK
    qb, fb, vb = qcol * A_DK // w, fcol * A_DK // w, vcol * A_DK // w
    grid_spec = pltpu.PrefetchScalarGridSpec(
        num_scalar_prefetch=1,
        grid=(A_HEADS // nh, nsteps),
        in_specs=[pl.BlockSpec((tb, w), lambda h, s, t: (t[0, s], qb + h)),
                  pl.BlockSpec((tb, w), lambda h, s, t: (t[0, s], fb + h)),
                  pl.BlockSpec((tb, w), lambda h, s, t: (t[0, s], vb + h)),
                  pl.BlockSpec((1, n_rec, w), lambda h, s, t: (d, 0, h)),
                  pl.BlockSpec((1, nh, A_DK, A_DV), lambda h, s, t: (t[1, s], h, 0, 0))],
        out_specs=[pl.BlockSpec((tb, w), lambda h, s, t: (t[0, s], h)),
                   pl.BlockSpec((1, nh, A_DK, A_DV), lambda h, s, t: (t[1, s], h, 0, 0))],
        scratch_shapes=[pltpu.VMEM((nh, A_DV, A_DK), F32)])
    return pl.pallas_call(
        functools.partial(_gla_kernel, reverse=reverse, layer_j=layer_j),
        grid_spec=grid_spec,
        out_shape=[jax.ShapeDtypeStruct((ntok, A_HEADS * A_DV), F32),
                   jax.ShapeDtypeStruct((nseq, A_HEADS, A_DK, A_DV), F32)],
        compiler_params=_cparams(("arbitrary", "arbitrary")),
        name="gla_bwd" if reverse else "gla_fwd",
    )(tbl, z, z, z, hg_lb_t, s0_all)


def _conv_kernel(tbl_ref, ua_ref, ug_ref, pa_ref, pg_ref, na_ref, ng_ref, w_ref, b_ref, lg_ref, lbias_ref, o_ref,
                 buf_ref):
    i = pl.program_id(0)
    tb = ua_ref.shape[0]
    hl = CONV_HALO
    prev_ok = tbl_ref[0, i] == 1
    next_ok = tbl_ref[1, i] == 1
    buf_ref[hl:hl + tb, :] = ua_ref[...] * jax.nn.sigmoid(ug_ref[...])
    buf_ref[0:hl, :] = jnp.where(prev_ok, pa_ref[...] * jax.nn.sigmoid(pg_ref[...]), 0.0)
    buf_ref[hl + tb:hl + tb + hl, :] = jnp.where(next_ok, na_ref[...] * jax.nn.sigmoid(ng_ref[...]), 0.0)
    w = w_ref[...]
    acc = jnp.zeros(o_ref.shape, F32) + b_ref[...]
    base = hl - CONV_W // 2
    for t in range(CONV_W):
        acc = acc + buf_ref[base + t:base + t + tb, :] * w[t:t + 1]
    mu = jnp.mean(acc, axis=-1, keepdims=True)
    xc = acc - mu
    var = jnp.mean(xc * xc, axis=-1, keepdims=True)
    y = xc * lax.rsqrt(var + EPS) * lg_ref[...] + lbias_ref[...]
    o_ref[...] = jax.nn.silu(y)


def _conv(z, tbl, cv_w, cv_b, ln_g, ln_b, *, acol, gcol):
    ntok = z.shape[0]
    ch = cv_w.shape[1]
    tb = TOK_BLK
    hl = CONV_HALO
    r = tb // hl
    last = ntok // hl - 1
    prev = lambda i, t: jnp.maximum(i * r - 1, 0)
    nxt = lambda i, t: jnp.minimum((i + 1) * r, last)
    grid_spec = pltpu.PrefetchScalarGridSpec(
        num_scalar_prefetch=1,
        grid=(ntok // tb,),
        in_specs=[pl.BlockSpec((tb, ch), lambda i, t: (i, acol)),
                  pl.BlockSpec((tb, ch), lambda i, t: (i, gcol)),
                  pl.BlockSpec((hl, ch), lambda i, t: (prev(i, t), acol)),
                  pl.BlockSpec((hl, ch), lambda i, t: (prev(i, t), gcol)),
                  pl.BlockSpec((hl, ch), lambda i, t: (nxt(i, t), acol)),
                  pl.BlockSpec((hl, ch), lambda i, t: (nxt(i, t), gcol)),
                  pl.BlockSpec((CONV_W, ch), lambda i, t: (0, 0)),
                  pl.BlockSpec((1, ch), lambda i, t: (0, 0)),
                  pl.BlockSpec((1, ch), lambda i, t: (0, 0)),
                  pl.BlockSpec((1, ch), lambda i, t: (0, 0))],
        out_specs=pl.BlockSpec((tb, ch), lambda i, t: (i, 0)),
        scratch_shapes=[pltpu.VMEM((tb + 2 * hl, ch), F32)])
    return pl.pallas_call(
        _conv_kernel,
        grid_spec=grid_spec,
        out_shape=jax.ShapeDtypeStruct((ntok, ch), F32),
        compiler_params=_cparams(("arbitrary",)),
        name="conv_module",
    )(tbl, z, z, z, z, z, z, cv_w, cv_b.reshape(1, ch), ln_g.reshape(1, ch), ln_b.reshape(1, ch))


def _even_out_kernel(x_ref, mod_ref, of_ref, ob_ref, zg_ref, oc_ref, gn_ref, w_ref, y_ref, *, gate_idx):
    o = of_ref[...] + ob_ref[...]
    gate = jax.nn.silu(zg_ref[...])
    gn = gn_ref[...]
    parts = []
    for h in range(A_HEADS):
        oh = o[:, h * A_DV:(h + 1) * A_DV]
        ms = jnp.mean(oh * oh, axis=-1, keepdims=True)
        parts.append(oh * lax.rsqrt(ms + EPS) * gn * gate[:, h * A_DV:(h + 1) * A_DV])
    o_a = jnp.concatenate(parts, axis=-1)
    na = o_a.shape[1]
    acc = _bdot(o_a, w_ref[0:na, :]) + _bdot(oc_ref[...], w_ref[na:, :])
    g = mod_ref[0][gate_idx:gate_idx + 1]
    y_ref[...] = x_ref[...] + g * acc


def _even_out(x, mod, o_f, o_b, z, o_c, gn_g, w_out, row_fn, tm, gate_idx, gcol):
    ntok, d = x.shape
    na = o_f.shape[1]
    nc = o_c.shape[1]
    return pl.pallas_call(
        functools.partial(_even_out_kernel, gate_idx=gate_idx),
        grid=(ntok // tm,),
        in_specs=[pl.BlockSpec((tm, d), lambda i: (i, 0)),
                  pl.BlockSpec((1, N_MOD, d), lambda i: (row_fn(i), 0, 0)),
                  pl.BlockSpec((tm, na), lambda i: (i, 0)),
                  pl.BlockSpec((tm, na), lambda i: (i, 0)),
                  pl.BlockSpec((tm, na), lambda i: (i, gcol)),
                  pl.BlockSpec((tm, nc), lambda i: (i, 0)),
                  pl.BlockSpec((1, A_DV), lambda i: (0, 0)),
                  pl.BlockSpec((na + nc, d), lambda i: (0, 0))],
        out_specs=pl.BlockSpec((tm, d), lambda i: (i, 0)),
        out_shape=jax.ShapeDtypeStruct((ntok, d), F32),
        compiler_params=_cparams(("arbitrary",)),
        name="even_out",
    )(x, mod, o_f, o_b, z, o_c, gn_g.reshape(1, A_DV), w_out)


def _lin_out_kernel(x_ref, mod_ref, o_ref, w_ref, y_ref, *, gate_idx):
    g = mod_ref[0][gate_idx:gate_idx + 1]
    y_ref[...] = x_ref[...] + g * _bdot(o_ref[...], w_ref[...])


def _lin_out(x, mod, o, w, row_fn, tm, gate_idx):
    ntok, d = x.shape
    kdim = o.shape[1]
    return pl.pallas_call(
        functools.partial(_lin_out_kernel, gate_idx=gate_idx),
        grid=(ntok // tm,),
        in_specs=[pl.BlockSpec((tm, d), lambda i: (i, 0)),
                  pl.BlockSpec((1, N_MOD, d), lambda i: (row_fn(i), 0, 0)),
                  pl.BlockSpec((tm, kdim), lambda i: (i, 0)),
                  pl.BlockSpec((kdim, d), lambda i: (0, 0))],
        out_specs=pl.BlockSpec((tm, d), lambda i: (i, 0)),
        out_shape=jax.ShapeDtypeStruct((ntok, d), F32),
        compiler_params=_cparams(("arbitrary",)),
        name="lin_out",
    )(x, mod, o, w)


def _ctx_attn_kernel(q_ref, k_ref, v_ref, o_ref):
    scale = C_HD ** -0.5
    for h in range(C_HEADS):
        sl = slice(h * C_HD, (h + 1) * C_HD)
        s = _bdot_nt(q_ref[:, sl], k_ref[:, sl]) * scale
        e = jnp.exp(s - jnp.max(s, axis=-1, keepdims=True))
        p = e / jnp.sum(e, axis=-1, keepdims=True)
        o_ref[:, sl] = _bdot(p, v_ref[:, sl])


def _ctx_attn(zqkv, n_prompt, seq):
    ntok = zqkv.shape[0]
    w = C_HEADS * C_HD
    return pl.pallas_call(
        _ctx_attn_kernel,
        grid=(n_prompt // seq,),
        in_specs=[pl.BlockSpec((seq, w), lambda b: (b, 0)),
                  pl.BlockSpec((seq, w), lambda b: (b, 1)),
                  pl.BlockSpec((seq, w), lambda b: (b, 2))],
        out_specs=pl.BlockSpec((seq, w), lambda b: (b, 0)),
        out_shape=jax.ShapeDtypeStruct((ntok, w), F32),
        compiler_params=_cparams(("arbitrary",)),
        name="ctx_attn",
    )(zqkv, zqkv, zqkv)


def _na_kernel(o_in_ref, q_ref, k_ref, v_ref, kc_ref, vc_ref, bias_ref, o_ref, *, rows):
    del o_in_ref
    scale = C_HD ** -0.5
    kh = min(WIN_H, rows)
    nloc = kh * GRID_W

    def body(rq, carry):
        r0 = jnp.clip(rq - kh // 2, 0, rows - kh)
        off = r0 - rq + (WIN_H - 1)
        qrow = q_ref[pl.ds(pl.multiple_of(rq * GRID_W, GRID_W), GRID_W), :]
        kwin = k_ref[pl.ds(pl.multiple_of(r0 * GRID_W, GRID_W), nloc), :]
        vwin = v_ref[pl.ds(pl.multiple_of(r0 * GRID_W, GRID_W), nloc), :]
        outs = []
        for hh in range(LANES // C_HD):
            sl = slice(hh * C_HD, (hh + 1) * C_HD)
            qh = qrow[:, sl]
            s_loc = _bdot_nt(qh, kwin[:, sl]) * scale + bias_ref[hh, off]
            s_ctx = _bdot_nt(qh, kc_ref[0, 0, hh]) * scale
            m = jnp.maximum(jnp.max(s_loc, axis=-1, keepdims=True), jnp.max(s_ctx, axis=-1, keepdims=True))
            e_loc = jnp.exp(s_loc - m)
            e_ctx = jnp.exp(s_ctx - m)
            den = jnp.sum(e_loc, axis=-1, keepdims=True) + jnp.sum(e_ctx, axis=-1, keepdims=True)
            inv = 1.0 / den
            outs.append(_bdot(e_loc * inv, vwin[:, sl]) + _bdot(e_ctx * inv, vc_ref[0, 0, hh]))
        o_ref[pl.ds(pl.multiple_of(rq * GRID_W, GRID_W), GRID_W), :] = jnp.concatenate(outs, axis=-1)
        return carry

    lax.fori_loop(0, rows, body, 0, unroll=2)


def _na_attn(o_prompt, zqkv, cache_k, cache_v, bias8, layer_j, n_prompt, dec_seq):
    ntok = zqkv.shape[0]
    nb = (ntok - n_prompt) // dec_seq
    rows = dec_seq // GRID_W
    boff = n_prompt // dec_seq
    hp = LANES // C_HD
    npair = C_HEADS // hp
    past = cache_k.shape[3]
    return pl.pallas_call(
        functools.partial(_na_kernel, rows=rows),
        grid=(npair, nb),
        in_specs=[pl.BlockSpec(memory_space=pl.ANY),
                  pl.BlockSpec((dec_seq, LANES), lambda p, b: (boff + b, p)),
                  pl.BlockSpec((dec_seq, LANES), lambda p, b: (boff + b, npair + p)),
                  pl.BlockSpec((dec_seq, LANES), lambda p, b: (boff + b, 2 * npair + p)),
                  pl.BlockSpec((1, 1, hp, past, C_HD), lambda p, b: (b, layer_j, p, 0, 0)),
                  pl.BlockSpec((1, 1, hp, past, C_HD), lambda p, b: (b, layer_j, p, 0, 0)),
                  pl.BlockSpec((hp, WIN_H, GRID_W, min(WIN_H, rows) * GRID_W), lambda p, b: (p, 0, 0, 0))],
        out_specs=pl.BlockSpec((dec_seq, LANES), lambda p, b: (boff + b, p)),
        out_shape=jax.ShapeDtypeStruct(o_prompt.shape, F32),
        input_output_aliases={0: 0},
        compiler_params=_cparams(("arbitrary", "arbitrary")),
        name="na_attn",
    )(o_prompt, zqkv, zqkv, zqkv, cache_k, cache_v, bias8)


def _na_bias_kernel(rpb_ref, e_ref, neg_ref, o_ref):
    h, m, l = _split3(rpb_ref[...])
    e = e_ref[...]
    d = lambda y: jnp.dot(y, e, preferred_element_type=F32)
    o_ref[...] = d(h) + (d(m) + d(l)) + neg_ref[...]


def _na_bias(rpb_l, rows):
    kh = min(WIN_H, rows)
    nh, ndr, ndc = rpb_l.shape
    cq = np.arange(GRID_W)
    c0 = np.clip(cq - WIN_W // 2, 0, GRID_W - WIN_W)
    in_win = (cq[None, :] >= c0[:, None]) & (cq[None, :] < c0[:, None] + WIN_W)
    dc = np.clip(cq[None, :] - cq[:, None] + (WIN_W - 1), 0, ndc - 1)
    ndc_pad = 32
    onehot = (dc[None] == np.arange(ndc_pad)[:, None, None]) & in_win[None]
    onehot = jnp.asarray(onehot.reshape(ndc_pad, GRID_W * GRID_W), BF16)
    neg = jnp.asarray(np.where(in_win, 0.0, -np.inf).reshape(1, GRID_W * GRID_W), F32)
    rpb2 = jnp.pad(rpb_l.astype(F32).reshape(nh * ndr, ndc), ((0, 0), (0, ndc_pad - ndc)))
    m = pl.pallas_call(
        _na_bias_kernel,
        out_shape=jax.ShapeDtypeStruct((nh * ndr, GRID_W * GRID_W), F32),
        name="na_bias",
    )(rpb2, onehot, neg).reshape(nh, ndr, GRID_W, GRID_W)
    per_off = [jnp.transpose(m[:, off:off + kh], (0, 2, 1, 3)).reshape(nh, GRID_W, kh * GRID_W)
               for off in range(WIN_H)]
    return jnp.stack(per_off, axis=1)


def _top_desc(x, n):
    vals = []
    for _ in range(n):
        m = jnp.max(x, axis=0, keepdims=True)
        vals.append(m)
        x = jnp.where(x == m, -jnp.inf, x)
    return vals


def _peer_gate_kernel(hq_ref, keys_ref, a_ref, b_ref, tau_ref):
    kq = PEER_TOPK
    for h in range(PEER_HEADS):
        s = []
        tops = []
        for p in range(2):
            c0 = (h * 2 + p) * N_KEYS
            sp = _dot3_nt(keys_ref[h, p], hq_ref[:, c0:c0 + N_KEYS])
            s.append(sp)
            tops.append(_top_desc(sp, kq + 1))
        t0 = jnp.concatenate(tops[0][:kq], axis=0)
        t1 = jnp.concatenate(tops[1][:kq], axis=0)
        ninf = jnp.full((4, t0.shape[1]), -jnp.inf, F32)
        cand = jnp.concatenate([
            tops[0][0] + t1,
            tops[0][1] + t1[0:8],
            tops[0][2] + t1[0:8],
            tops[0][3] + t1[0:8],
            t0[8:16] + tops[1][0],
            t0[4:8] + tops[1][0], t0[4:8] + tops[1][1],
            t0[4:8] + tops[1][2], ninf], axis=0)
        best = _top_desc(cand, kq + 1)
        mx = best[0]
        z = jnp.zeros_like(mx)
        for r in range(kq):
            z = z + jnp.exp(best[r] - mx)
        nxt = jnp.maximum(best[kq], jnp.maximum(tops[0][kq] + tops[1][0], tops[0][0] + tops[1][kq]))
        thr = 0.5 * (best[kq - 1] + nxt)
        inv_z = 1.0 / z
        a_ref[h] = jnp.exp(s[0] - tops[0][0]) * inv_z
        b_ref[h] = jnp.exp(s[1] - tops[1][0]).astype(BF16)
        tau_ref[h:h + 1, :] = jnp.exp(thr - mx) * inv_z


def _peer_gates(hq, keys, tmk):
    ntok = hq.shape[0]
    return pl.pallas_call(
        _peer_gate_kernel,
        grid=(ntok // tmk,),
        in_specs=[pl.BlockSpec((tmk, hq.shape[1]), lambda i: (i, 0)),
                  pl.BlockSpec(keys.shape, lambda i: (0, 0, 0, 0))],
        out_specs=[pl.BlockSpec((PEER_HEADS, N_KEYS, tmk), lambda i: (0, 0, i)),
                   pl.BlockSpec((PEER_HEADS, N_KEYS, tmk), lambda i: (0, 0, i)),
                   pl.BlockSpec((PEER_HEADS, tmk), lambda i: (0, i))],
        out_shape=[jax.ShapeDtypeStruct((PEER_HEADS, N_KEYS, ntok), F32),
                   jax.ShapeDtypeStruct((PEER_HEADS, N_KEYS, ntok), BF16),
                   jax.ShapeDtypeStruct((PEER_HEADS, ntok), F32)],
        compiler_params=_cparams(("arbitrary",)),
        name="peer_gates",
    )(hq, keys)


BF16_ROWS = 16
PEER_TOK_SUB = 256


def _peer_mix_kernel(x_ref, g_ref, mod_ref, u_ref, vt_ref, a_ref, b_ref, tau_ref, y_ref, h_ref, acc_ref, act_ref,
                     ht_ref, *, shift_idx, scale_idx, gate_idx):
    e = pl.program_id(1)
    te = u_ref.shape[0]
    tm = x_ref.shape[0]
    nsub = te // N_KEYS
    nq = tm // PEER_TOK_SUB

    @pl.when(e == 0)
    def _():
        m = mod_ref[0]
        h = _modnorm(x_ref[...], g_ref[...], m[shift_idx:shift_idx + 1], m[scale_idx:scale_idx + 1])
        h_ref[...] = h.T.astype(BF16)
        acc_ref[...] = jnp.zeros_like(acc_ref)

    def activations(q):
        cs = slice(q * PEER_TOK_SUB, (q + 1) * PEER_TOK_SUB)
        act_ref[q % 2] = jnp.dot(u_ref[...], h_ref[:, cs], preferred_element_type=F32)

    activations(0)
    for q in range(nq):
        cs = slice(q * PEER_TOK_SUB, (q + 1) * PEER_TOK_SUB)
        if q + 1 < nq:
            activations(q + 1)
        grp = (N_KEYS // BF16_ROWS, BF16_ROWS, PEER_TOK_SUB)
        bcast = lambda r: jnp.broadcast_to(r, (BF16_ROWS, PEER_TOK_SUB)).astype(BF16)[None]
        taus = [bcast(tau_ref[h:h + 1, cs]) for h in range(PEER_HEADS)]
        zero = jnp.zeros(grp, BF16)
        for ii in range(nsub):
            i = e * nsub + ii
            rows = slice(ii * N_KEYS, (ii + 1) * N_KEYS)
            gsum = None
            for h in range(PEER_HEADS):
                w = bcast(a_ref[h, pl.ds(i, 1), cs]) * b_ref[h, :, cs].reshape(grp)
                term = jnp.where(w >= taus[h], w, zero)
                gsum = term if gsum is None else gsum + term
            act = jax.nn.gelu(act_ref[q % 2, rows, :]).astype(BF16).reshape(grp)
            ht_ref[q % 2, rows, :] = (act * gsum).reshape(N_KEYS, PEER_TOK_SUB)
        acc_ref[:, cs] += jnp.dot(vt_ref[...], ht_ref[q % 2], preferred_element_type=F32)

    @pl.when(e == pl.num_programs(1) - 1)
    def _():
        g = mod_ref[0][gate_idx:gate_idx + 1]
        y_ref[...] = x_ref[...] + g * acc_ref[...].T


def _peer_mix(x, g, mod, u_bf, vt_bf, a, b, tau, row_fn, tm, te, shift_idx, scale_idx, gate_idx):
    ntok, d = x.shape
    nexp = u_bf.shape[0]
    return pl.pallas_call(
        functools.partial(_peer_mix_kernel, shift_idx=shift_idx, scale_idx=scale_idx, gate_idx=gate_idx),
        grid=(ntok // tm, nexp // te),
        in_specs=[pl.BlockSpec((tm, d), lambda t, e: (t, 0)),
                  pl.BlockSpec((1, d), lambda t, e: (0, 0)),
                  pl.BlockSpec((1, N_MOD, d), lambda t, e: (row_fn(t), 0, 0)),
                  pl.BlockSpec((te, d), lambda t, e: (e, 0)),
                  pl.BlockSpec((d, te), lambda t, e: (0, e)),
                  pl.BlockSpec((PEER_HEADS, N_KEYS, tm), lambda t, e: (0, 0, t)),
                  pl.BlockSpec((PEER_HEADS, N_KEYS, tm), lambda t, e: (0, 0, t)),
                  pl.BlockSpec((PEER_HEADS, tm), lambda t, e: (0, t))],
        out_specs=pl.BlockSpec((tm, d), lambda t, e: (t, 0)),
        out_shape=jax.ShapeDtypeStruct((ntok, d), F32),
        scratch_shapes=[pltpu.VMEM((d, tm), BF16),
                        pltpu.VMEM((d, tm), F32),
                        pltpu.VMEM((2, te, PEER_TOK_SUB), F32),
                        pltpu.VMEM((2, te, PEER_TOK_SUB), BF16)],
        compiler_params=_cparams(("arbitrary", "arbitrary")),
        name="peer_mix",
    )(x, g.reshape(1, d), mod, u_bf, vt_bf, a, b, tau)


def _seq_tables(n_prompt, seq, n_dec, dec_seq, tb):
    seq_of, first, last = [], [], []
    sid = 0
    for count, length in ((n_prompt // seq, seq), (n_dec, dec_seq)):
        nb = length // tb
        for _ in range(count):
            for b in range(nb):
                seq_of.append(sid)
                first.append(int(b == 0))
                last.append(int(b == nb - 1))
            sid += 1
    nblk = len(seq_of)
    fwd = np.array([list(range(nblk)), seq_of, first], np.int32)
    order = list(range(nblk - 1, -1, -1))
    bwd = np.array([order, [seq_of[i] for i in order], [last[i] for i in order]], np.int32)
    halo = np.array([[1 - f for f in first], [1 - l for l in last]], np.int32)
    return fwd, bwd, halo


def kernel(x_prompt, x_sample, cache_k, cache_v, state_fwd, state_bwd, c, c_ctx, norm_g, w_mod, b_mod, w_in_ab, hg_lb,
           hg_norm_g, cv_w, cv_b, cv_ln_g, cv_ln_b, w_out_ab, w_qkv, qk_g, rpb, w_out_na, peer_wq, peer_keys, peer_u,
           peer_v):
    bp, seq, d = x_prompt.shape
    nd, dec_seq, _ = x_sample.shape
    depth = w_mod.shape[0]
    n_prompt = bp * seq
    ntok = n_prompt + nd * dec_seq
    assert n_prompt % dec_seq == 0 and seq % TOK_BLK == 0 and dec_seq % TOK_BLK == 0
    rows = dec_seq // GRID_W
    n_qk = A_HEADS * A_DK
    n_v = A_HEADS * A_DV
    n_b = d - n_v

    tile = lambda limit: max(t for t in (128, 256, 512, 1024) if t <= limit and n_prompt % t == 0 and dec_seq % t == 0)
    tm_lin = tile(1024)
    tm_out = tile(512)
    tm_peer = tile(1024)
    te_peer = 512
    tmk = 256

    x = jnp.concatenate([x_prompt.reshape(n_prompt, d), x_sample.reshape(nd * dec_seq, d)], axis=0)
    cond8 = jnp.zeros((8, d), F32).at[0].set(c_ctx).at[1:1 + nd].set(c)
    mods = _modulation(cond8, w_mod, b_mod).reshape(depth, 8, N_MOD, d)

    fwd_tbl, bwd_tbl, halo_tbl = _seq_tables(n_prompt, seq, nd, dec_seq, TOK_BLK)
    fwd_tbl, bwd_tbl, halo_tbl = jnp.asarray(fwd_tbl), jnp.asarray(bwd_tbl), jnp.asarray(halo_tbl)
    hg_lb_t = jnp.transpose(hg_lb, (1, 0, 2))
    zeros_state = jnp.zeros((bp, A_HEADS, A_DK, A_DV), F32)

    row = lambda tm: _mod_row(n_prompt, dec_seq, tm)
    tn_qkv = 512
    bd = (jnp.arange(tn_qkv)[:, None] // C_HD == jnp.arange(tn_qkv)[None, :] // C_HD).astype(BF16) * (1.0 / C_HD)

    new_k, new_v, new_sf, new_sb = [], [], [], []
    for l in range(depth):
        j = l // 2
        mod = mods[l]
        if l % 2 == 0:
            z = _normlin(x, norm_g[l, 0], mod, w_in_ab[j], row(tm_lin), tm_lin, 512, 0, 1)
            blk = lambda col: col // A_DK
            s0f = jnp.concatenate([zeros_state, state_fwd[:, j]], axis=0)
            s0b = jnp.concatenate([zeros_state, state_bwd[:, j]], axis=0)
            o_f, sf = _gla(z, hg_lb_t, s0f, fwd_tbl, reverse=False, layer_j=j,
                           qcol=0, fcol=blk(n_qk), vcol=blk(3 * n_qk))
            o_b, sb = _gla(z, hg_lb_t, s0b, bwd_tbl, reverse=True, layer_j=j,
                           qcol=0, fcol=blk(2 * n_qk), vcol=blk(3 * n_qk))
            new_sf.append(sf[:bp])
            new_sb.append(sb[:bp])
            cbase = 3 * n_qk + 2 * n_v
            o_c = _conv(z, halo_tbl, cv_w[j], cv_b[j], cv_ln_g[j], cv_ln_b[j],
                        acol=cbase // n_b, gcol=(cbase + n_b) // n_b)
            x = _even_out(x, mod, o_f, o_b, z, o_c, hg_norm_g[j], w_out_ab[j], row(tm_out), tm_out, 2,
                          (3 * n_qk + n_v) // n_v)
        else:
            wq = C_HEADS * C_HD
            qg = jnp.concatenate([jnp.tile(qk_g[j, 0], C_HEADS), jnp.tile(qk_g[j, 1], C_HEADS)]).reshape(1, 2 * wq)
            z = _normlin(x, norm_g[l, 0], mod, w_qkv[j], row(tm_lin), tm_lin, tn_qkv, 0, 1, qk=(bd, qg, 2 * wq))
            kv = z[:n_prompt].reshape(bp, seq, 3, C_HEADS, C_HD)
            new_k.append(jnp.transpose(kv[:, :, 1], (0, 2, 1, 3)))
            new_v.append(jnp.transpose(kv[:, :, 2], (0, 2, 1, 3)))
            o = _ctx_attn(z, n_prompt, seq)
            o = _na_attn(o, z, cache_k, cache_v, _na_bias(rpb[j], rows), j, n_prompt, dec_seq)
            x = _lin_out(x, mod, o, w_out_na[j], row(tm_out), tm_out, 2)
        hq = _normlin(x, norm_g[l, 1], mod, peer_wq[l], row(tm_lin), tm_lin, 512, 3, 4, precise=True)
        a, b, tau = _peer_gates(hq, peer_keys[l], tmk)
        x = _peer_mix(x, norm_g[l, 1], mod, peer_u[l].astype(BF16), peer_v[l].T.astype(BF16), a, b, tau,
                      row(tm_peer), tm_peer, te_peer, 3, 4, 5)

    yp = x[:n_prompt].reshape(bp, seq, d)
    ys = x[n_prompt:].reshape(nd, dec_seq, d)
    return (yp, ys, jnp.stack(new_k, axis=1), jnp.stack(new_v, axis=1),
            jnp.stack(new_sf, axis=1), jnp.stack(new_sb, axis=1))
```

```python
import functools

import numpy as np
import jax
import jax.numpy as jnp
from jax import lax
from jax.experimental import pallas as pl
from jax.experimental.pallas import tpu as pltpu

F32 = jnp.float32
BF16 = jnp.bfloat16

GRID_W = 64
A_HEADS = 4
A_DK = 128
A_DV = 128
CONV_W = 31
GLA_CHUNK = 16
C_HEADS = 16
C_HD = 64
WIN_H = 8
WIN_W = 16
PEER_HEADS = 8
N_KEYS = 128
PEER_TOPK = 16
N_MOD = 6
EPS = 1e-6

LANES = 128
TOK_BLK = 256
CONV_HALO = 16
VMEM_LIMIT = 56 * 1024 * 1024


def _cparams(sem):
    return pltpu.CompilerParams(dimension_semantics=sem, vmem_limit_bytes=VMEM_LIMIT)


def _bdot(a, b):
    return jnp.dot(a.astype(BF16), b.astype(BF16), preferred_element_type=F32)


def _bdot_nt(a, b):
    return lax.dot_general(a.astype(BF16), b.astype(BF16), (((1,), (1,)), ((), ())), preferred_element_type=F32)


def _split2(a):
    hi = a.astype(BF16)
    lo = (a - hi.astype(F32)).astype(BF16)
    return hi, lo


def _split3(a):
    hi = a.astype(BF16)
    r = a - hi.astype(F32)
    mid = r.astype(BF16)
    lo = (r - mid.astype(F32)).astype(BF16)
    return hi, mid, lo


def _dot3(a, b):
    ah, al = _split2(a)
    bh, bl = _split2(b)
    d = lambda x, y: jnp.dot(x, y, preferred_element_type=F32)
    return d(ah, bh) + (d(ah, bl) + d(al, bh))


def _dot3_nt(a, b):
    ah, al = _split2(a)
    bh, bl = _split2(b)
    d = lambda x, y: lax.dot_general(x, y, (((1,), (1,)), ((), ())), preferred_element_type=F32)
    return d(ah, bh) + (d(ah, bl) + d(al, bh))


def _exact_left(mask_bf16, x):
    h, m, l = _split3(x)
    d = lambda y: jnp.dot(mask_bf16, y, preferred_element_type=F32)
    return d(h) + (d(m) + d(l))


def _mod_row(n_prompt, dec_seq, tm):
    def row(i):
        tok = i * tm
        return jnp.where(tok < n_prompt, 0, 1 + (tok - n_prompt) // dec_seq)
    return row


def _modnorm(x, g, shift, scale):
    ms = jnp.mean(x * x, axis=-1, keepdims=True)
    return x * lax.rsqrt(ms + EPS) * g * (1.0 + scale) + shift


def _mod_kernel(c_ref, w_ref, b_ref, o_ref):
    s = jax.nn.silu(c_ref[...])
    o_ref[0] = _dot3(s, w_ref[0]) + b_ref[0]


def _modulation(cond8, w_mod, b_mod):
    depth, d, n = w_mod.shape
    tn = 1536
    return pl.pallas_call(
        _mod_kernel,
        grid=(depth, n // tn),
        in_specs=[pl.BlockSpec((8, d), lambda l, j: (0, 0)),
                  pl.BlockSpec((1, d, tn), lambda l, j: (l, 0, j)),
                  pl.BlockSpec((1, 1, tn), lambda l, j: (l, 0, j))],
        out_specs=pl.BlockSpec((1, 8, tn), lambda l, j: (l, 0, j)),
        out_shape=jax.ShapeDtypeStruct((depth, 8, n), F32),
        compiler_params=_cparams(("arbitrary", "arbitrary")),
        name="modulation",
    )(cond8, w_mod, b_mod.reshape(depth, 1, n))


def _normlin_kernel(x_ref, g_ref, mod_ref, w_ref, *rest, shift_idx, scale_idx, precise, qk_tiles):
    if qk_tiles:
        bd_ref, qg_ref, o_ref, h_ref = rest
    else:
        o_ref, h_ref = rest
    j = pl.program_id(1)

    @pl.when(j == 0)
    def _():
        m = mod_ref[0]
        h = _modnorm(x_ref[...], g_ref[...], m[shift_idx:shift_idx + 1], m[scale_idx:scale_idx + 1])
        if precise:
            hi, lo = _split2(h)
            h_ref[0] = hi
            h_ref[1] = lo
        else:
            h_ref[0] = h.astype(BF16)

    w = w_ref[...]
    if precise:
        wh, wl = _split2(w)
        d = lambda a, b: jnp.dot(a, b, preferred_element_type=F32)
        acc = d(h_ref[0], wh) + (d(h_ref[0], wl) + d(h_ref[1], wh))
    else:
        acc = jnp.dot(h_ref[0], w.astype(BF16), preferred_element_type=F32)

    if qk_tiles:
        @pl.when(j < qk_tiles)
        def _():
            sh, sl = _split2(acc * acc)
            bd = bd_ref[...]
            ms = jnp.dot(sh, bd, preferred_element_type=F32) + jnp.dot(sl, bd, preferred_element_type=F32)
            o_ref[...] = acc * lax.rsqrt(ms + EPS) * qg_ref[...]

        @pl.when(j >= qk_tiles)
        def _():
            o_ref[...] = acc
    else:
        o_ref[...] = acc


def _normlin(x, g, mod, w, row_fn, tm, tn, shift_idx, scale_idx, precise=False, qk=None):
    ntok, d = x.shape
    n = w.shape[1]
    in_specs = [pl.BlockSpec((tm, d), lambda i, j: (i, 0)),
                pl.BlockSpec((1, d), lambda i, j: (0, 0)),
                pl.BlockSpec((1, N_MOD, d), lambda i, j: (row_fn(i), 0, 0)),
                pl.BlockSpec((d, tn), lambda i, j: (0, j))]
    args = [x, g.reshape(1, d), mod, w]
    qk_tiles = 0
    if qk is not None:
        bd, qg, n_qk = qk
        qk_tiles = n_qk // tn
        in_specs += [pl.BlockSpec((tn, tn), lambda i, j: (0, 0)),
                     pl.BlockSpec((1, tn), lambda i, j: (0, jnp.minimum(j, qk_tiles - 1)))]
        args += [bd, qg]
    return pl.pallas_call(
        functools.partial(_normlin_kernel, shift_idx=shift_idx, scale_idx=scale_idx, precise=precise,
                          qk_tiles=qk_tiles),
        grid=(ntok // tm, n // tn),
        in_specs=in_specs,
        out_specs=pl.BlockSpec((tm, tn), lambda i, j: (i, j)),
        out_shape=jax.ShapeDtypeStruct((ntok, n), F32),
        scratch_shapes=[pltpu.VMEM((2 if precise else 1, tm, d), BF16)],
        compiler_params=_cparams(("arbitrary", "arbitrary")),
        name="normlin",
    )(*args)


GLA_HEADS_PER_STEP = 2


def _gla_kernel(tbl_ref, zq_ref, zf_ref, zv_ref, lb_ref, s0_ref, o_ref, sfin_ref, st_ref, *, reverse, layer_j):
    step = pl.program_id(1)
    tb = zq_ref.shape[0]
    nchunk = tb // GLA_CHUNK
    nh = zq_ref.shape[1] // A_DK

    @pl.when(tbl_ref[2, step] == 1)
    def _():
        for hh in range(nh):
            st_ref[hh] = s0_ref[0, hh].T

    row = lax.broadcasted_iota(jnp.int32, (tb, tb), 0)
    col = lax.broadcasted_iota(jnp.int32, (tb, tb), 1)
    same = (row // GLA_CHUNK) == (col // GLA_CHUNK)
    if reverse:
        incl = same & (col >= row)
        rest = same & (col < row)
    else:
        incl = same & (col <= row)
        rest = same & (col > row)
    incl_m = jnp.where(incl, 1.0, 0.0).astype(BF16)
    rest_m = jnp.where(rest, 1.0, 0.0).astype(BF16)
    ridx = lax.broadcasted_iota(jnp.int32, (tb, A_DK), 0) // GLA_CHUNK
    cidx_l = lax.broadcasted_iota(jnp.int32, (A_DV, tb), 1) // GLA_CHUNK

    lbx = lb_ref[0]
    e = jnp.exp(lbx - jnp.max(lbx, axis=0, keepdims=True))
    p = e / jnp.sum(e, axis=0, keepdims=True)
    lb_all = jnp.zeros((1, lbx.shape[1]), F32)
    for i in range(1, layer_j + 1):
        lb_all = lb_all + p[i:i + 1]

    for hh in range(nh):
        sl = slice(hh * A_DK, (hh + 1) * A_DK)
        lb = lb_all[:, sl]
        z = zf_ref[:, sl]
        log_sig = jnp.minimum(z, 0.0) - jnp.log1p(jnp.exp(-jnp.abs(z)))
        a_ = jnp.log1p(-lb) + log_sig
        b_ = jnp.log(lb)
        logf = jnp.maximum(a_, b_) + jnp.log1p(jnp.exp(-jnp.abs(a_ - b_)))
        k = (1.0 - lb) * jax.nn.sigmoid(-z)
        q = zq_ref[:, sl] * (A_DK ** -0.5)
        v = zv_ref[:, sl]

        bc = _exact_left(incl_m, logf)
        brem = _exact_left(rest_m, logf)
        qd = q * jnp.exp(bc)
        kinv = k * jnp.exp(-bc)
        att = jnp.where(incl, _bdot_nt(qd, kinv), 0.0)
        o_intra = _bdot(att, v)
        klast = k * jnp.exp(brem)
        vt = v.T
        vt_blocks = jnp.concatenate([jnp.where(cidx_l == c, vt, 0.0) for c in range(nchunk)], axis=0)
        upd = _bdot(vt_blocks, klast)
        st = st_ref[hh]
        starts = [None] * nchunk
        order = range(nchunk - 1, -1, -1) if reverse else range(nchunk)
        for c in order:
            lo = c * GLA_CHUNK
            starts[c] = st
            last = lo if reverse else lo + GLA_CHUNK - 1
            a_c = jnp.exp(bc[last:last + 1])
            st = a_c * st + upd[c * A_DV:(c + 1) * A_DV]
        qd_blocks = jnp.concatenate([jnp.where(ridx == c, qd, 0.0) for c in range(nchunk)], axis=1)
        o_inter = _bdot_nt(qd_blocks, jnp.concatenate(starts, axis=1))
        st_ref[hh] = st
        o_ref[:, sl] = o_intra + o_inter
        sfin_ref[0, hh] = st.T


def _gla(z, hg_lb_t, s0_all, tbl, *, reverse, layer_j, qcol, fcol, vcol):
    ntok = z.shape[0]
    nseq = s0_all.shape[0]
    nsteps = tbl.shape[1]
    tb = TOK_BLK
    n_rec = hg_lb_t.shape[1]
    d = 1 if reverse else 0
    nh = GLA_HEADS_PER_STEP
    w = nh * A_DK
    qb, fb, vb = qcol * A_DK // w, fcol * A_DK // w, vcol * A_DK // w
    grid_spec = pltpu.PrefetchScalarGridSpec(
        num_scalar_prefetch=1,
        grid=(A_HEADS // nh, nsteps),
        in_specs=[pl.BlockSpec((tb, w), lambda h, s, t: (t[0, s], qb + h)),
                  pl.BlockSpec((tb, w), lambda h, s, t: (t[0, s], fb + h)),
                  pl.BlockSpec((tb, w), lambda h, s, t: (t[0, s], vb + h)),
                  pl.BlockSpec((1, n_rec, w), lambda h, s, t: (d, 0, h)),
                  pl.BlockSpec((1, nh, A_DK, A_DV), lambda h, s, t: (t[1, s], h, 0, 0))],
        out_specs=[pl.BlockSpec((tb, w), lambda h, s, t: (t[0, s], h)),
                   pl.BlockSpec((1, nh, A_DK, A_DV), lambda h, s, t: (t[1, s], h, 0, 0))],
        scratch_shapes=[pltpu.VMEM((nh, A_DV, A_DK), F32)])
    return pl.pallas_call(
        functools.partial(_gla_kernel, reverse=reverse, layer_j=layer_j),
        grid_spec=grid_spec,
        out_shape=[jax.ShapeDtypeStruct((ntok, A_HEADS * A_DV), F32),
                   jax.ShapeDtypeStruct((nseq, A_HEADS, A_DK, A_DV), F32)],
        compiler_params=_cparams(("arbitrary", "arbitrary")),
        name="gla_bwd" if reverse else "gla_fwd",
    )(tbl, z, z, z, hg_lb_t, s0_all)


def _conv_kernel(tbl_ref, ua_ref, ug_ref, pa_ref, pg_ref, na_ref, ng_ref, w_ref, b_ref, lg_ref, lbias_ref, o_ref,
                 buf_ref):
    i = pl.program_id(0)
    tb = ua_ref.shape[0]
    hl = CONV_HALO
    prev_ok = tbl_ref[0, i] == 1
    next_ok = tbl_ref[1, i] == 1
    buf_ref[hl:hl + tb, :] = ua_ref[...] * jax.nn.sigmoid(ug_ref[...])
    buf_ref[0:hl, :] = jnp.where(prev_ok, pa_ref[...] * jax.nn.sigmoid(pg_ref[...]), 0.0)
    buf_ref[hl + tb:hl + tb + hl, :] = jnp.where(next_ok, na_ref[...] * jax.nn.sigmoid(ng_ref[...]), 0.0)
    w = w_ref[...]
    acc = jnp.zeros(o_ref.shape, F32) + b_ref[...]
    base = hl - CONV_W // 2
    for t in range(CONV_W):
        acc = acc + buf_ref[base + t:base + t + tb, :] * w[t:t + 1]
    mu = jnp.mean(acc, axis=-1, keepdims=True)
    xc = acc - mu
    var = jnp.mean(xc * xc, axis=-1, keepdims=True)
    y = xc * lax.rsqrt(var + EPS) * lg_ref[...] + lbias_ref[...]
    o_ref[...] = jax.nn.silu(y)


def _conv(z, tbl, cv_w, cv_b, ln_g, ln_b, *, acol, gcol):
    ntok = z.shape[0]
    ch = cv_w.shape[1]
    tb = TOK_BLK
    hl = CONV_HALO
    r = tb // hl
    last = ntok // hl - 1
    prev = lambda i, t: jnp.maximum(i * r - 1, 0)
    nxt = lambda i, t: jnp.minimum((i + 1) * r, last)
    grid_spec = pltpu.PrefetchScalarGridSpec(
        num_scalar_prefetch=1,
        grid=(ntok // tb,),
        in_specs=[pl.BlockSpec((tb, ch), lambda i, t: (i, acol)),
                  pl.BlockSpec((tb, ch), lambda i, t: (i, gcol)),
                  pl.BlockSpec((hl, ch), lambda i, t: (prev(i, t), acol)),
                  pl.BlockSpec((hl, ch), lambda i, t: (prev(i, t), gcol)),
                  pl.BlockSpec((hl, ch), lambda i, t: (nxt(i, t), acol)),
                  pl.BlockSpec((hl, ch), lambda i, t: (nxt(i, t), gcol)),
                  pl.BlockSpec((CONV_W, ch), lambda i, t: (0, 0)),
                  pl.BlockSpec((1, ch), lambda i, t: (0, 0)),
                  pl.BlockSpec((1, ch), lambda i, t: (0, 0)),
                  pl.BlockSpec((1, ch), lambda i, t: (0, 0))],
        out_specs=pl.BlockSpec((tb, ch), lambda i, t: (i, 0)),
        scratch_shapes=[pltpu.VMEM((tb + 2 * hl, ch), F32)])
    return pl.pallas_call(
        _conv_kernel,
        grid_spec=grid_spec,
        out_shape=jax.ShapeDtypeStruct((ntok, ch), F32),
        compiler_params=_cparams(("arbitrary",)),
        name="conv_module",
    )(tbl, z, z, z, z, z, z, cv_w, cv_b.reshape(1, ch), ln_g.reshape(1, ch), ln_b.reshape(1, ch))


def _even_out_kernel(x_ref, mod_ref, of_ref, ob_ref, zg_ref, oc_ref, gn_ref, w_ref, y_ref, *, gate_idx):
    o = of_ref[...] + ob_ref[...]
    gate = jax.nn.silu(zg_ref[...])
    gn = gn_ref[...]
    parts = []
    for h in range(A_HEADS):
        oh = o[:, h * A_DV:(h + 1) * A_DV]
        ms = jnp.mean(oh * oh, axis=-1, keepdims=True)
        parts.append(oh * lax.rsqrt(ms + EPS) * gn * gate[:, h * A_DV:(h + 1) * A_DV])
    o_a = jnp.concatenate(parts, axis=-1)
    na = o_a.shape[1]
    acc = _bdot(o_a, w_ref[0:na, :]) + _bdot(oc_ref[...], w_ref[na:, :])
    g = mod_ref[0][gate_idx:gate_idx + 1]
    y_ref[...] = x_ref[...] + g * acc


def _even_out(x, mod, o_f, o_b, z, o_c, gn_g, w_out, row_fn, tm, gate_idx, gcol):
    ntok, d = x.shape
    na = o_f.shape[1]
    nc = o_c.shape[1]
    return pl.pallas_call(
        functools.partial(_even_out_kernel, gate_idx=gate_idx),
        grid=(ntok // tm,),
        in_specs=[pl.BlockSpec((tm, d), lambda i: (i, 0)),
                  pl.BlockSpec((1, N_MOD, d), lambda i: (row_fn(i), 0, 0)),
                  pl.BlockSpec((tm, na), lambda i: (i, 0)),
                  pl.BlockSpec((tm, na), lambda i: (i, 0)),
                  pl.BlockSpec((tm, na), lambda i: (i, gcol)),
                  pl.BlockSpec((tm, nc), lambda i: (i, 0)),
                  pl.BlockSpec((1, A_DV), lambda i: (0, 0)),
                  pl.BlockSpec((na + nc, d), lambda i: (0, 0))],
        out_specs=pl.BlockSpec((tm, d), lambda i: (i, 0)),
        out_shape=jax.ShapeDtypeStruct((ntok, d), F32),
        compiler_params=_cparams(("arbitrary",)),
        name="even_out",
    )(x, mod, o_f, o_b, z, o_c, gn_g.reshape(1, A_DV), w_out)


def _lin_out_kernel(x_ref, mod_ref, o_ref, w_ref, y_ref, *, gate_idx):
    g = mod_ref[0][gate_idx:gate_idx + 1]
    y_ref[...] = x_ref[...] + g * _bdot(o_ref[...], w_ref[...])


def _lin_out(x, mod, o, w, row_fn, tm, gate_idx):
    ntok, d = x.shape
    kdim = o.shape[1]
    return pl.pallas_call(
        functools.partial(_lin_out_kernel, gate_idx=gate_idx),
        grid=(ntok // tm,),
        in_specs=[pl.BlockSpec((tm, d), lambda i: (i, 0)),
                  pl.BlockSpec((1, N_MOD, d), lambda i: (row_fn(i), 0, 0)),
                  pl.BlockSpec((tm, kdim), lambda i: (i, 0)),
                  pl.BlockSpec((kdim, d), lambda i: (0, 0))],
        out_specs=pl.BlockSpec((tm, d), lambda i: (i, 0)),
        out_shape=jax.ShapeDtypeStruct((ntok, d), F32),
        compiler_params=_cparams(("arbitrary",)),
        name="lin_out",
    )(x, mod, o, w)


def _ctx_attn_kernel(q_ref, k_ref, v_ref, o_ref):
    scale = C_HD ** -0.5
    for h in range(C_HEADS):
        sl = slice(h * C_HD, (h + 1) * C_HD)
        s = _bdot_nt(q_ref[:, sl], k_ref[:, sl]) * scale
        e = jnp.exp(s - jnp.max(s, axis=-1, keepdims=True))
        p = e / jnp.sum(e, axis=-1, keepdims=True)
        o_ref[:, sl] = _bdot(p, v_ref[:, sl])


def _ctx_attn(zqkv, n_prompt, seq):
    ntok = zqkv.shape[0]
    w = C_HEADS * C_HD
    return pl.pallas_call(
        _ctx_attn_kernel,
        grid=(n_prompt // seq,),
        in_specs=[pl.BlockSpec((seq, w), lambda b: (b, 0)),
                  pl.BlockSpec((seq, w), lambda b: (b, 1)),
                  pl.BlockSpec((seq, w), lambda b: (b, 2))],
        out_specs=pl.BlockSpec((seq, w), lambda b: (b, 0)),
        out_shape=jax.ShapeDtypeStruct((ntok, w), F32),
        compiler_params=_cparams(("arbitrary",)),
        name="ctx_attn",
    )(zqkv, zqkv, zqkv)


NA_QROWS = 8
NA_KROWS = 16


def _na_blocks(rows):
    kh = min(WIN_H, rows)
    assert kh == WIN_H and rows % NA_QROWS == 0 and rows >= NA_KROWS
    blocks, types = [], []
    for m in range(rows // NA_QROWS):
        k0 = int(np.clip(NA_QROWS * m - kh // 2, 0, rows - NA_KROWS))
        tbl = np.full((NA_QROWS, NA_KROWS), 2 * WIN_H - 1, np.int64)
        for ql in range(NA_QROWS):
            rq = NA_QROWS * m + ql
            r0 = int(np.clip(rq - kh // 2, 0, rows - kh))
            for kl in range(NA_KROWS):
                rk = k0 + kl
                if r0 <= rk < r0 + kh:
                    tbl[ql, kl] = rk - rq + (WIN_H - 1)
        key = tbl.tobytes()
        if key not in [t.tobytes() for t in types]:
            types.append(tbl)
        blocks.append((NA_QROWS * m, k0, [t.tobytes() for t in types].index(key)))
    return blocks, types


def _na_kernel(o_in_ref, q_ref, k_ref, v_ref, kc_ref, vc_ref, bias_ref, o_ref, *, blocks):
    del o_in_ref
    scale = C_HD ** -0.5
    for hh in range(LANES // C_HD):
        sl = slice(hh * C_HD, (hh + 1) * C_HD)
        kc = kc_ref[0, 0, hh]
        vc = vc_ref[0, 0, hh]
        for q0, k0, t in blocks:
            qs = slice(q0 * GRID_W, (q0 + NA_QROWS) * GRID_W)
            ks = slice(k0 * GRID_W, (k0 + NA_KROWS) * GRID_W)
            qb = q_ref[qs, sl]
            s_loc = _bdot_nt(qb, k_ref[ks, sl]) * scale + bias_ref[hh, t]
            s_ctx = _bdot_nt(qb, kc) * scale
            m = jnp.maximum(jnp.max(s_loc, axis=-1, keepdims=True), jnp.max(s_ctx, axis=-1, keepdims=True))
            e_loc = jnp.exp(s_loc - m)
            e_ctx = jnp.exp(s_ctx - m)
            den = jnp.sum(e_loc, axis=-1, keepdims=True) + jnp.sum(e_ctx, axis=-1, keepdims=True)
            o = _bdot(e_loc, v_ref[ks, sl]) + _bdot(e_ctx, vc)
            o_ref[qs, sl] = o / den


def _na_attn(o_prompt, zqkv, cache_k, cache_v, bias, blocks, layer_j, n_prompt, dec_seq):
    ntok = zqkv.shape[0]
    nb = (ntok - n_prompt) // dec_seq
    boff = n_prompt // dec_seq
    hp = LANES // C_HD
    npair = C_HEADS // hp
    past = cache_k.shape[3]
    return pl.pallas_call(
        functools.partial(_na_kernel, blocks=blocks),
        grid=(npair, nb),
        in_specs=[pl.BlockSpec(memory_space=pl.ANY),
                  pl.BlockSpec((dec_seq, LANES), lambda p, b: (boff + b, p)),
                  pl.BlockSpec((dec_seq, LANES), lambda p, b: (boff + b, npair + p)),
                  pl.BlockSpec((dec_seq, LANES), lambda p, b: (boff + b, 2 * npair + p)),
                  pl.BlockSpec((1, 1, hp, past, C_HD), lambda p, b: (b, layer_j, p, 0, 0)),
                  pl.BlockSpec((1, 1, hp, past, C_HD), lambda p, b: (b, layer_j, p, 0, 0)),
                  pl.BlockSpec((hp,) + bias.shape[1:], lambda p, b: (p, 0, 0, 0))],
        out_specs=pl.BlockSpec((dec_seq, LANES), lambda p, b: (boff + b, p)),
        out_shape=jax.ShapeDtypeStruct(o_prompt.shape, F32),
        input_output_aliases={0: 0},
        compiler_params=_cparams(("arbitrary", "arbitrary")),
        name="na_attn",
    )(o_prompt, zqkv, zqkv, zqkv, cache_k, cache_v, bias)


def _na_bias_kernel(rpb_ref, e_ref, neg_ref, o_ref):
    h, m, l = _split3(rpb_ref[...])
    e = e_ref[...]
    d = lambda y: jnp.dot(y, e, preferred_element_type=F32)
    o_ref[...] = d(h) + (d(m) + d(l)) + neg_ref[...]


def _na_bias(rpb_l, types):
    nh, ndr, ndc = rpb_l.shape
    cq = np.arange(GRID_W)
    c0 = np.clip(cq - WIN_W // 2, 0, GRID_W - WIN_W)
    in_win = (cq[None, :] >= c0[:, None]) & (cq[None, :] < c0[:, None] + WIN_W)
    dc = np.clip(cq[None, :] - cq[:, None] + (WIN_W - 1), 0, ndc - 1)
    ndc_pad = 32
    onehot = (dc[None] == np.arange(ndc_pad)[:, None, None]) & in_win[None]
    onehot = jnp.asarray(onehot.reshape(ndc_pad, GRID_W * GRID_W), BF16)
    neg = jnp.asarray(np.where(in_win, 0.0, -np.inf).reshape(1, GRID_W * GRID_W), F32)
    rpb2 = jnp.pad(rpb_l.astype(F32).reshape(nh * ndr, ndc), ((0, 0), (0, ndc_pad - ndc)))
    m = pl.pallas_call(
        _na_bias_kernel,
        out_shape=jax.ShapeDtypeStruct((nh * ndr, GRID_W * GRID_W), F32),
        name="na_bias",
    )(rpb2, onehot, neg).reshape(nh, ndr, GRID_W, GRID_W)
    tiles = [m[:, dr] for dr in range(ndr)] + [jnp.full((nh, GRID_W, GRID_W), -jnp.inf, F32)]
    per_type = [jnp.concatenate([jnp.concatenate([tiles[int(d)] for d in row], axis=-1) for row in tbl], axis=-2)
                for tbl in types]
    return jnp.stack(per_type, axis=1)


def _top_desc(x, n):
    vals = []
    for _ in range(n):
        m = jnp.max(x, axis=0, keepdims=True)
        vals.append(m)
        x = jnp.where(x == m, -jnp.inf, x)
    return vals


def _peer_gate_kernel(hq_ref, keys_ref, a_ref, b_ref, tau_ref):
    kq = PEER_TOPK
    for h in range(PEER_HEADS):
        s = []
        tops = []
        for p in range(2):
            c0 = (h * 2 + p) * N_KEYS
            sp = _dot3_nt(keys_ref[h, p], hq_ref[:, c0:c0 + N_KEYS])
            s.append(sp)
            tops.append(_top_desc(sp, kq + 1))
        t0 = jnp.concatenate(tops[0][:kq], axis=0)
        t1 = jnp.concatenate(tops[1][:kq], axis=0)
        ninf = jnp.full((4, t0.shape[1]), -jnp.inf, F32)
        cand = jnp.concatenate([
            tops[0][0] + t1,
            tops[0][1] + t1[0:8],
            tops[0][2] + t1[0:8],
            tops[0][3] + t1[0:8],
            t0[8:16] + tops[1][0],
            t0[4:8] + tops[1][0], t0[4:8] + tops[1][1],
            t0[4:8] + tops[1][2], ninf], axis=0)
        best = _top_desc(cand, kq + 1)
        mx = best[0]
        z = jnp.zeros_like(mx)
        for r in range(kq):
            z = z + jnp.exp(best[r] - mx)
        nxt = jnp.maximum(best[kq], jnp.maximum(tops[0][kq] + tops[1][0], tops[0][0] + tops[1][kq]))
        thr = 0.5 * (best[kq - 1] + nxt)
        inv_z = 1.0 / z
        a_ref[h] = jnp.exp(s[0] - tops[0][0]) * inv_z
        b_ref[h] = jnp.exp(s[1] - tops[1][0]).astype(BF16)
        tau_ref[h:h + 1, :] = jnp.exp(thr - mx) * inv_z


def _peer_gates(hq, keys, tmk):
    ntok = hq.shape[0]
    return pl.pallas_call(
        _peer_gate_kernel,
        grid=(ntok // tmk,),
        in_specs=[pl.BlockSpec((tmk, hq.shape[1]), lambda i: (i, 0)),
                  pl.BlockSpec(keys.shape, lambda i: (0, 0, 0, 0))],
        out_specs=[pl.BlockSpec((PEER_HEADS, N_KEYS, tmk), lambda i: (0, 0, i)),
                   pl.BlockSpec((PEER_HEADS, N_KEYS, tmk), lambda i: (0, 0, i)),
                   pl.BlockSpec((PEER_HEADS, tmk), lambda i: (0, i))],
        out_shape=[jax.ShapeDtypeStruct((PEER_HEADS, N_KEYS, ntok), F32),
                   jax.ShapeDtypeStruct((PEER_HEADS, N_KEYS, ntok), BF16),
                   jax.ShapeDtypeStruct((PEER_HEADS, ntok), F32)],
        compiler_params=_cparams(("arbitrary",)),
        name="peer_gates",
    )(hq, keys)


BF16_ROWS = 16
PEER_TOK_SUB = 256


def _peer_mix_kernel(x_ref, g_ref, mod_ref, u_ref, vt_ref, a_ref, b_ref, tau_ref, y_ref, h_ref, acc_ref, act_ref,
                     ht_ref, *, shift_idx, scale_idx, gate_idx):
    e = pl.program_id(1)
    te = u_ref.shape[0]
    tm = x_ref.shape[0]
    nsub = te // N_KEYS
    nq = tm // PEER_TOK_SUB

    @pl.when(e == 0)
    def _():
        m = mod_ref[0]
        h = _modnorm(x_ref[...], g_ref[...], m[shift_idx:shift_idx + 1], m[scale_idx:scale_idx + 1])
        h_ref[...] = h.T.astype(BF16)
        acc_ref[...] = jnp.zeros_like(acc_ref)

    def activations(q):
        cs = slice(q * PEER_TOK_SUB, (q + 1) * PEER_TOK_SUB)
        act_ref[q % 2] = jnp.dot(u_ref[...], h_ref[:, cs], preferred_element_type=F32)

    activations(0)
    for q in range(nq):
        cs = slice(q * PEER_TOK_SUB, (q + 1) * PEER_TOK_SUB)
        if q + 1 < nq:
            activations(q + 1)
        grp = (N_KEYS // BF16_ROWS, BF16_ROWS, LANES)
        bcast = lambda r: jnp.broadcast_to(r, (BF16_ROWS, LANES)).astype(BF16)[None]
        zero = jnp.zeros(grp, BF16)
        nlb = PEER_TOK_SUB // LANES
        lanes = [slice(lb * LANES, (lb + 1) * LANES) for lb in range(nlb)]
        taus = [[bcast(tau_ref[h:h + 1, cs][:, sub]) for h in range(PEER_HEADS)] for sub in lanes]
        for ii in range(nsub):
            i = e * nsub + ii
            rows = slice(ii * N_KEYS, (ii + 1) * N_KEYS)
            arows = [a_ref[h, pl.ds(i, 1), cs] for h in range(PEER_HEADS)]
            for lb, sub in enumerate(lanes):
                cl = slice(q * PEER_TOK_SUB + lb * LANES, q * PEER_TOK_SUB + (lb + 1) * LANES)
                gsum = None
                for h in range(PEER_HEADS):
                    w = bcast(arows[h][:, sub]) * b_ref[h, :, cl].reshape(grp)
                    term = jnp.where(w >= taus[lb][h], w, zero)
                    gsum = term if gsum is None else gsum + term
                act = jax.nn.gelu(act_ref[q % 2, rows, sub].astype(BF16)).reshape(grp)
                ht_ref[q % 2, rows, sub] = (act * gsum).reshape(N_KEYS, LANES)
        acc_ref[:, cs] += jnp.dot(vt_ref[...], ht_ref[q % 2], preferred_element_type=F32)

    @pl.when(e == pl.num_programs(1) - 1)
    def _():
        g = mod_ref[0][gate_idx:gate_idx + 1]
        y_ref[...] = x_ref[...] + g * acc_ref[...].T


def _peer_mix(x, g, mod, u_bf, vt_bf, a, b, tau, row_fn, tm, te, shift_idx, scale_idx, gate_idx):
    ntok, d = x.shape
    nexp = u_bf.shape[0]
    return pl.pallas_call(
        functools.partial(_peer_mix_kernel, shift_idx=shift_idx, scale_idx=scale_idx, gate_idx=gate_idx),
        grid=(ntok // tm, nexp // te),
        in_specs=[pl.BlockSpec((tm, d), lambda t, e: (t, 0)),
                  pl.BlockSpec((1, d), lambda t, e: (0, 0)),
                  pl.BlockSpec((1, N_MOD, d), lambda t, e: (row_fn(t), 0, 0)),
                  pl.BlockSpec((te, d), lambda t, e: (e, 0)),
                  pl.BlockSpec((d, te), lambda t, e: (0, e)),
                  pl.BlockSpec((PEER_HEADS, N_KEYS, tm), lambda t, e: (0, 0, t)),
                  pl.BlockSpec((PEER_HEADS, N_KEYS, tm), lambda t, e: (0, 0, t)),
                  pl.BlockSpec((PEER_HEADS, tm), lambda t, e: (0, t))],
        out_specs=pl.BlockSpec((tm, d), lambda t, e: (t, 0)),
        out_shape=jax.ShapeDtypeStruct((ntok, d), F32),
        scratch_shapes=[pltpu.VMEM((d, tm), BF16),
                        pltpu.VMEM((d, tm), F32),
                        pltpu.VMEM((2, te, PEER_TOK_SUB), F32),
                        pltpu.VMEM((2, te, PEER_TOK_SUB), BF16)],
        compiler_params=_cparams(("arbitrary", "arbitrary")),
        name="peer_mix",
    )(x, g.reshape(1, d), mod, u_bf, vt_bf, a, b, tau)


def _seq_tables(n_prompt, seq, n_dec, dec_seq, tb):
    seq_of, first, last = [], [], []
    sid = 0
    for count, length in ((n_prompt // seq, seq), (n_dec, dec_seq)):
        nb = length // tb
        for _ in range(count):
            for b in range(nb):
                seq_of.append(sid)
                first.append(int(b == 0))
                last.append(int(b == nb - 1))
            sid += 1
    nblk = len(seq_of)
    fwd = np.array([list(range(nblk)), seq_of, first], np.int32)
    order = list(range(nblk - 1, -1, -1))
    bwd = np.array([order, [seq_of[i] for i in order], [last[i] for i in order]], np.int32)
    halo = np.array([[1 - f for f in first], [1 - l for l in last]], np.int32)
    return fwd, bwd, halo


def kernel(x_prompt, x_sample, cache_k, cache_v, state_fwd, state_bwd, c, c_ctx, norm_g, w_mod, b_mod, w_in_ab, hg_lb,
           hg_norm_g, cv_w, cv_b, cv_ln_g, cv_ln_b, w_out_ab, w_qkv, qk_g, rpb, w_out_na, peer_wq, peer_keys, peer_u,
           peer_v):
    bp, seq, d = x_prompt.shape
    nd, dec_seq, _ = x_sample.shape
    depth = w_mod.shape[0]
    n_prompt = bp * seq
    ntok = n_prompt + nd * dec_seq
    assert n_prompt % dec_seq == 0 and seq % TOK_BLK == 0 and dec_seq % TOK_BLK == 0
    rows = dec_seq // GRID_W
    n_qk = A_HEADS * A_DK
    n_v = A_HEADS * A_DV
    n_b = d - n_v

    tile = lambda limit: max(t for t in (128, 256, 512, 1024) if t <= limit and n_prompt % t == 0 and dec_seq % t == 0)
    tm_lin = tile(1024)
    tm_out = tile(512)
    tm_peer = tile(1024)
    te_peer = 1024
    tmk = 256

    x = jnp.concatenate([x_prompt.reshape(n_prompt, d), x_sample.reshape(nd * dec_seq, d)], axis=0)
    cond8 = jnp.zeros((8, d), F32).at[0].set(c_ctx).at[1:1 + nd].set(c)
    mods = _modulation(cond8, w_mod, b_mod).reshape(depth, 8, N_MOD, d)

    fwd_tbl, bwd_tbl, halo_tbl = _seq_tables(n_prompt, seq, nd, dec_seq, TOK_BLK)
    fwd_tbl, bwd_tbl, halo_tbl = jnp.asarray(fwd_tbl), jnp.asarray(bwd_tbl), jnp.asarray(halo_tbl)
    hg_lb_t = jnp.transpose(hg_lb, (1, 0, 2))
    zeros_state = jnp.zeros((bp, A_HEADS, A_DK, A_DV), F32)

    row = lambda tm: _mod_row(n_prompt, dec_seq, tm)
    tn_qkv = 512
    bd = (jnp.arange(tn_qkv)[:, None] // C_HD == jnp.arange(tn_qkv)[None, :] // C_HD).astype(BF16) * (1.0 / C_HD)

    new_k, new_v, new_sf, new_sb = [], [], [], []
    for l in range(depth):
        j = l // 2
        mod = mods[l]
        if l % 2 == 0:
            z = _normlin(x, norm_g[l, 0], mod, w_in_ab[j], row(tm_lin), tm_lin, 512, 0, 1)
            blk = lambda col: col // A_DK
            s0f = jnp.concatenate([zeros_state, state_fwd[:, j]], axis=0)
            s0b = jnp.concatenate([zeros_state, state_bwd[:, j]], axis=0)
            o_f, sf = _gla(z, hg_lb_t, s0f, fwd_tbl, reverse=False, layer_j=j,
                           qcol=0, fcol=blk(n_qk), vcol=blk(3 * n_qk))
            o_b, sb = _gla(z, hg_lb_t, s0b, bwd_tbl, reverse=True, layer_j=j,
                           qcol=0, fcol=blk(2 * n_qk), vcol=blk(3 * n_qk))
            new_sf.append(sf[:bp])
            new_sb.append(sb[:bp])
            cbase = 3 * n_qk + 2 * n_v
            o_c = _conv(z, halo_tbl, cv_w[j], cv_b[j], cv_ln_g[j], cv_ln_b[j],
                        acol=cbase // n_b, gcol=(cbase + n_b) // n_b)
            x = _even_out(x, mod, o_f, o_b, z, o_c, hg_norm_g[j], w_out_ab[j], row(tm_out), tm_out, 2,
                          (3 * n_qk + n_v) // n_v)
        else:
            wq = C_HEADS * C_HD
            qg = jnp.concatenate([jnp.tile(qk_g[j, 0], C_HEADS), jnp.tile(qk_g[j, 1], C_HEADS)]).reshape(1, 2 * wq)
            z = _normlin(x, norm_g[l, 0], mod, w_qkv[j], row(tm_lin), tm_lin, tn_qkv, 0, 1, qk=(bd, qg, 2 * wq))
            kv = z[:n_prompt].reshape(bp, seq, 3, C_HEADS, C_HD)
            new_k.append(jnp.transpose(kv[:, :, 1], (0, 2, 1, 3)))
            new_v.append(jnp.transpose(kv[:, :, 2], (0, 2, 1, 3)))
            o = _ctx_attn(z, n_prompt, seq)
            na_blocks, na_types = _na_blocks(rows)
            o = _na_attn(o, z, cache_k, cache_v, _na_bias(rpb[j], na_types), na_blocks, j, n_prompt, dec_seq)
            x = _lin_out(x, mod, o, w_out_na[j], row(tm_out), tm_out, 2)
        hq = _normlin(x, norm_g[l, 1], mod, peer_wq[l], row(tm_lin), tm_lin, 512, 3, 4, precise=True)
        a, b, tau = _peer_gates(hq, peer_keys[l], tmk)
        x = _peer_mix(x, norm_g[l, 1], mod, peer_u[l].astype(BF16), peer_v[l].T.astype(BF16), a, b, tau,
                      row(tm_peer), tm_peer, te_peer, 3, 4, 5)

    yp = x[:n_prompt].reshape(bp, seq, d)
    ys = x[n_prompt:].reshape(nd, dec_seq, d)
    return (yp, ys, jnp.stack(new_k, axis=1), jnp.stack(new_v, axis=1),
            jnp.stack(new_sf, axis=1), jnp.stack(new_sb, axis=1))
```

```python
import functools

import numpy as np
import jax
import jax.numpy as jnp
from jax import lax
from jax.experimental import pallas as pl
from jax.experimental.pallas import tpu as pltpu

F32 = jnp.float32
BF16 = jnp.bfloat16

GRID_W = 64
A_HEADS = 4
A_DK = 128
A_DV = 128
CONV_W = 31
GLA_CHUNK = 16
C_HEADS = 16
C_HD = 64
WIN_H = 8
WIN_W = 16
PEER_HEADS = 8
N_KEYS = 128
PEER_TOPK = 16
N_MOD = 6
EPS = 1e-6

LANES = 128
TOK_BLK = 256
CONV_HALO = 16
VMEM_LIMIT = 56 * 1024 * 1024


def _cparams(sem):
    return pltpu.CompilerParams(dimension_semantics=sem, vmem_limit_bytes=VMEM_LIMIT)


def _bdot(a, b):
    return jnp.dot(a.astype(BF16), b.astype(BF16), preferred_element_type=F32)


def _bdot_nt(a, b):
    return lax.dot_general(a.astype(BF16), b.astype(BF16), (((1,), (1,)), ((), ())), preferred_element_type=F32)


def _split2(a):
    hi = a.astype(BF16)
    lo = (a - hi.astype(F32)).astype(BF16)
    return hi, lo


def _split3(a):
    hi = a.astype(BF16)
    r = a - hi.astype(F32)
    mid = r.astype(BF16)
    lo = (r - mid.astype(F32)).astype(BF16)
    return hi, mid, lo


def _dot3(a, b):
    ah, al = _split2(a)
    bh, bl = _split2(b)
    d = lambda x, y: jnp.dot(x, y, preferred_element_type=F32)
    return d(ah, bh) + (d(ah, bl) + d(al, bh))


def _dot3_nt(a, b):
    ah, al = _split2(a)
    bh, bl = _split2(b)
    d = lambda x, y: lax.dot_general(x, y, (((1,), (1,)), ((), ())), preferred_element_type=F32)
    return d(ah, bh) + (d(ah, bl) + d(al, bh))


def _exact_left(mask_bf16, x):
    h, m, l = _split3(x)
    d = lambda y: jnp.dot(mask_bf16, y, preferred_element_type=F32)
    return d(h) + (d(m) + d(l))


def _mod_row(n_prompt, dec_seq, tm):
    def row(i):
        tok = i * tm
        return jnp.where(tok < n_prompt, 0, 1 + (tok - n_prompt) // dec_seq)
    return row


def _modnorm(x, g, shift, scale):
    ms = jnp.mean(x * x, axis=-1, keepdims=True)
    return x * lax.rsqrt(ms + EPS) * g * (1.0 + scale) + shift


def _mod_kernel(c_ref, w_ref, b_ref, o_ref):
    s = jax.nn.silu(c_ref[...])
    o_ref[0] = _dot3(s, w_ref[0]) + b_ref[0]


def _modulation(cond8, w_mod, b_mod):
    depth, d, n = w_mod.shape
    tn = 1536
    return pl.pallas_call(
        _mod_kernel,
        grid=(depth, n // tn),
        in_specs=[pl.BlockSpec((8, d), lambda l, j: (0, 0)),
                  pl.BlockSpec((1, d, tn), lambda l, j: (l, 0, j)),
                  pl.BlockSpec((1, 1, tn), lambda l, j: (l, 0, j))],
        out_specs=pl.BlockSpec((1, 8, tn), lambda l, j: (l, 0, j)),
        out_shape=jax.ShapeDtypeStruct((depth, 8, n), F32),
        compiler_params=_cparams(("arbitrary", "arbitrary")),
        name="modulation",
    )(cond8, w_mod, b_mod.reshape(depth, 1, n))


def _normlin_kernel(x_ref, g_ref, mod_ref, w_ref, *rest, shift_idx, scale_idx, qk_tiles):
    if qk_tiles:
        bd_ref, qg_ref, o_ref, h_ref = rest
    else:
        o_ref, h_ref = rest
    j = pl.program_id(1)

    @pl.when(j == 0)
    def _():
        m = mod_ref[0]
        h = _modnorm(x_ref[...], g_ref[...], m[shift_idx:shift_idx + 1], m[scale_idx:scale_idx + 1])
        h_ref[...] = h.astype(BF16)

    acc = jnp.dot(h_ref[...], w_ref[...].astype(BF16), preferred_element_type=F32)

    if qk_tiles:
        @pl.when(j < qk_tiles)
        def _():
            sh, sl = _split2(acc * acc)
            bd = bd_ref[...]
            ms = jnp.dot(sh, bd, preferred_element_type=F32) + jnp.dot(sl, bd, preferred_element_type=F32)
            o_ref[...] = acc * lax.rsqrt(ms + EPS) * qg_ref[...]

        @pl.when(j >= qk_tiles)
        def _():
            o_ref[...] = acc
    else:
        o_ref[...] = acc


def _normlin(x, g, mod, w, row_fn, tm, tn, shift_idx, scale_idx, qk=None):
    ntok, d = x.shape
    n = w.shape[1]
    in_specs = [pl.BlockSpec((tm, d), lambda i, j: (i, 0)),
                pl.BlockSpec((1, d), lambda i, j: (0, 0)),
                pl.BlockSpec((1, N_MOD, d), lambda i, j: (row_fn(i), 0, 0)),
                pl.BlockSpec((d, tn), lambda i, j: (0, j))]
    args = [x, g.reshape(1, d), mod, w]
    qk_tiles = 0
    if qk is not None:
        bd, qg, n_qk = qk
        qk_tiles = n_qk // tn
        in_specs += [pl.BlockSpec((tn, tn), lambda i, j: (0, 0)),
                     pl.BlockSpec((1, tn), lambda i, j: (0, jnp.minimum(j, qk_tiles - 1)))]
        args += [bd, qg]
    return pl.pallas_call(
        functools.partial(_normlin_kernel, shift_idx=shift_idx, scale_idx=scale_idx, qk_tiles=qk_tiles),
        grid=(ntok // tm, n // tn),
        in_specs=in_specs,
        out_specs=pl.BlockSpec((tm, tn), lambda i, j: (i, j)),
        out_shape=jax.ShapeDtypeStruct((ntok, n), F32),
        scratch_shapes=[pltpu.VMEM((tm, d), BF16)],
        compiler_params=_cparams(("arbitrary", "arbitrary")),
        name="normlin",
    )(*args)


GLA_HEADS_PER_STEP = 2


def _gla_kernel(tbl_ref, zq_ref, zf_ref, zv_ref, lb_ref, s0_ref, o_ref, sfin_ref, st_ref, *, reverse, layer_j):
    step = pl.program_id(1)
    tb = zq_ref.shape[0]
    nchunk = tb // GLA_CHUNK
    nh = zq_ref.shape[1] // A_DK

    @pl.when((tbl_ref[2, step] == 1) & (tbl_ref[4, step] == 1))
    def _():
        for hh in range(nh):
            st_ref[hh] = s0_ref[0, 0, hh].T

    @pl.when((tbl_ref[2, step] == 1) & (tbl_ref[4, step] == 0))
    def _():
        st_ref[...] = jnp.zeros_like(st_ref)

    row = lax.broadcasted_iota(jnp.int32, (tb, tb), 0)
    col = lax.broadcasted_iota(jnp.int32, (tb, tb), 1)
    same = (row // GLA_CHUNK) == (col // GLA_CHUNK)
    if reverse:
        incl = same & (col >= row)
        rest = same & (col < row)
    else:
        incl = same & (col <= row)
        rest = same & (col > row)
    incl_m = jnp.where(incl, 1.0, 0.0).astype(BF16)
    rest_m = jnp.where(rest, 1.0, 0.0).astype(BF16)
    ridx = lax.broadcasted_iota(jnp.int32, (tb, A_DK), 0) // GLA_CHUNK
    cidx_l = lax.broadcasted_iota(jnp.int32, (A_DV, tb), 1) // GLA_CHUNK

    lbx = lb_ref[0]
    e = jnp.exp(lbx - jnp.max(lbx, axis=0, keepdims=True))
    p = e / jnp.sum(e, axis=0, keepdims=True)
    lb_all = jnp.zeros((1, lbx.shape[1]), F32)
    for i in range(1, layer_j + 1):
        lb_all = lb_all + p[i:i + 1]

    for hh in range(nh):
        sl = slice(hh * A_DK, (hh + 1) * A_DK)
        lb = lb_all[:, sl]
        z = zf_ref[:, sl]
        log_sig = jnp.minimum(z, 0.0) - jnp.log1p(jnp.exp(-jnp.abs(z)))
        a_ = jnp.log1p(-lb) + log_sig
        b_ = jnp.log(lb)
        logf = jnp.maximum(a_, b_) + jnp.log1p(jnp.exp(-jnp.abs(a_ - b_)))
        k = (1.0 - lb) * jax.nn.sigmoid(-z)
        q = zq_ref[:, sl] * (A_DK ** -0.5)
        v = zv_ref[:, sl]

        bc = _exact_left(incl_m, logf)
        brem = _exact_left(rest_m, logf)
        qd = q * jnp.exp(bc)
        kinv = k * jnp.exp(-bc)
        att = jnp.where(incl, _bdot_nt(qd, kinv), 0.0)
        o_intra = _bdot(att, v)
        klast = k * jnp.exp(brem)
        vt = v.T
        vt_blocks = jnp.concatenate([jnp.where(cidx_l == c, vt, 0.0) for c in range(nchunk)], axis=0)
        upd = _bdot(vt_blocks, klast)
        st = st_ref[hh]
        starts = [None] * nchunk
        order = range(nchunk - 1, -1, -1) if reverse else range(nchunk)
        for c in order:
            lo = c * GLA_CHUNK
            starts[c] = st
            last = lo if reverse else lo + GLA_CHUNK - 1
            a_c = jnp.exp(bc[last:last + 1])
            st = a_c * st + upd[c * A_DV:(c + 1) * A_DV]
        qd_blocks = jnp.concatenate([jnp.where(ridx == c, qd, 0.0) for c in range(nchunk)], axis=1)
        o_inter = _bdot_nt(qd_blocks, jnp.concatenate(starts, axis=1))
        st_ref[hh] = st
        o_ref[:, sl] = o_intra + o_inter
        sfin_ref[0, hh] = st.T


def _gla(z, hg_lb_t, s0, tbl, nseq, *, reverse, layer_j, qcol, fcol, vcol):
    ntok = z.shape[0]
    nsteps = tbl.shape[1]
    tb = TOK_BLK
    n_rec = hg_lb_t.shape[1]
    d = 1 if reverse else 0
    nh = GLA_HEADS_PER_STEP
    w = nh * A_DK
    qb, fb, vb = qcol * A_DK // w, fcol * A_DK // w, vcol * A_DK // w
    grid_spec = pltpu.PrefetchScalarGridSpec(
        num_scalar_prefetch=1,
        grid=(A_HEADS // nh, nsteps),
        in_specs=[pl.BlockSpec((tb, w), lambda h, s, t: (t[0, s], qb + h)),
                  pl.BlockSpec((tb, w), lambda h, s, t: (t[0, s], fb + h)),
                  pl.BlockSpec((tb, w), lambda h, s, t: (t[0, s], vb + h)),
                  pl.BlockSpec((1, n_rec, w), lambda h, s, t: (d, 0, h)),
                  pl.BlockSpec((1, 1, nh, A_DK, A_DV), lambda h, s, t: (t[3, s], layer_j, h, 0, 0))],
        out_specs=[pl.BlockSpec((tb, w), lambda h, s, t: (t[0, s], h)),
                   pl.BlockSpec((1, nh, A_DK, A_DV), lambda h, s, t: (t[1, s], h, 0, 0))],
        scratch_shapes=[pltpu.VMEM((nh, A_DV, A_DK), F32)])
    return pl.pallas_call(
        functools.partial(_gla_kernel, reverse=reverse, layer_j=layer_j),
        grid_spec=grid_spec,
        out_shape=[jax.ShapeDtypeStruct((ntok, A_HEADS * A_DV), F32),
                   jax.ShapeDtypeStruct((nseq, A_HEADS, A_DK, A_DV), F32)],
        compiler_params=_cparams(("arbitrary", "arbitrary")),
        name="gla_bwd" if reverse else "gla_fwd",
    )(tbl, z, z, z, hg_lb_t, s0)


def _conv_kernel(tbl_ref, ua_ref, ug_ref, pa_ref, pg_ref, na_ref, ng_ref, w_ref, b_ref, lg_ref, lbias_ref, o_ref,
                 buf_ref):
    i = pl.program_id(0)
    tb = ua_ref.shape[0]
    hl = CONV_HALO
    prev_ok = tbl_ref[0, i] == 1
    next_ok = tbl_ref[1, i] == 1
    buf_ref[hl:hl + tb, :] = ua_ref[...] * jax.nn.sigmoid(ug_ref[...])
    buf_ref[0:hl, :] = jnp.where(prev_ok, pa_ref[...] * jax.nn.sigmoid(pg_ref[...]), 0.0)
    buf_ref[hl + tb:hl + tb + hl, :] = jnp.where(next_ok, na_ref[...] * jax.nn.sigmoid(ng_ref[...]), 0.0)
    w = w_ref[...]
    acc = jnp.zeros(o_ref.shape, F32) + b_ref[...]
    base = hl - CONV_W // 2
    for t in range(CONV_W):
        acc = acc + buf_ref[base + t:base + t + tb, :] * w[t:t + 1]
    mu = jnp.mean(acc, axis=-1, keepdims=True)
    xc = acc - mu
    var = jnp.mean(xc * xc, axis=-1, keepdims=True)
    y = xc * lax.rsqrt(var + EPS) * lg_ref[...] + lbias_ref[...]
    o_ref[...] = jax.nn.silu(y)


def _conv(z, tbl, cv_w, cv_b, ln_g, ln_b, *, acol, gcol):
    ntok = z.shape[0]
    ch = cv_w.shape[1]
    tb = TOK_BLK
    hl = CONV_HALO
    r = tb // hl
    last = ntok // hl - 1
    prev = lambda i, t: jnp.maximum(i * r - 1, 0)
    nxt = lambda i, t: jnp.minimum((i + 1) * r, last)
    grid_spec = pltpu.PrefetchScalarGridSpec(
        num_scalar_prefetch=1,
        grid=(ntok // tb,),
        in_specs=[pl.BlockSpec((tb, ch), lambda i, t: (i, acol)),
                  pl.BlockSpec((tb, ch), lambda i, t: (i, gcol)),
                  pl.BlockSpec((hl, ch), lambda i, t: (prev(i, t), acol)),
                  pl.BlockSpec((hl, ch), lambda i, t: (prev(i, t), gcol)),
                  pl.BlockSpec((hl, ch), lambda i, t: (nxt(i, t), acol)),
                  pl.BlockSpec((hl, ch), lambda i, t: (nxt(i, t), gcol)),
                  pl.BlockSpec((CONV_W, ch), lambda i, t: (0, 0)),
                  pl.BlockSpec((1, ch), lambda i, t: (0, 0)),
                  pl.BlockSpec((1, ch), lambda i, t: (0, 0)),
                  pl.BlockSpec((1, ch), lambda i, t: (0, 0))],
        out_specs=pl.BlockSpec((tb, ch), lambda i, t: (i, 0)),
        scratch_shapes=[pltpu.VMEM((tb + 2 * hl, ch), F32)])
    return pl.pallas_call(
        _conv_kernel,
        grid_spec=grid_spec,
        out_shape=jax.ShapeDtypeStruct((ntok, ch), F32),
        compiler_params=_cparams(("arbitrary",)),
        name="conv_module",
    )(tbl, z, z, z, z, z, z, cv_w, cv_b.reshape(1, ch), ln_g.reshape(1, ch), ln_b.reshape(1, ch))


def _even_out_kernel(x_ref, mod_ref, of_ref, ob_ref, zg_ref, oc_ref, gn_ref, w_ref, y_ref, *, gate_idx):
    o = of_ref[...] + ob_ref[...]
    gate = jax.nn.silu(zg_ref[...])
    gn = gn_ref[...]
    parts = []
    for h in range(A_HEADS):
        oh = o[:, h * A_DV:(h + 1) * A_DV]
        ms = jnp.mean(oh * oh, axis=-1, keepdims=True)
        parts.append(oh * lax.rsqrt(ms + EPS) * gn * gate[:, h * A_DV:(h + 1) * A_DV])
    o_a = jnp.concatenate(parts, axis=-1)
    na = o_a.shape[1]
    acc = _bdot(o_a, w_ref[0:na, :]) + _bdot(oc_ref[...], w_ref[na:, :])
    g = mod_ref[0][gate_idx:gate_idx + 1]
    y_ref[...] = x_ref[...] + g * acc


def _even_out(x, mod, o_f, o_b, z, o_c, gn_g, w_out, row_fn, tm, gate_idx, gcol):
    ntok, d = x.shape
    na = o_f.shape[1]
    nc = o_c.shape[1]
    return pl.pallas_call(
        functools.partial(_even_out_kernel, gate_idx=gate_idx),
        grid=(ntok // tm,),
        in_specs=[pl.BlockSpec((tm, d), lambda i: (i, 0)),
                  pl.BlockSpec((1, N_MOD, d), lambda i: (row_fn(i), 0, 0)),
                  pl.BlockSpec((tm, na), lambda i: (i, 0)),
                  pl.BlockSpec((tm, na), lambda i: (i, 0)),
                  pl.BlockSpec((tm, na), lambda i: (i, gcol)),
                  pl.BlockSpec((tm, nc), lambda i: (i, 0)),
                  pl.BlockSpec((1, A_DV), lambda i: (0, 0)),
                  pl.BlockSpec((na + nc, d), lambda i: (0, 0))],
        out_specs=pl.BlockSpec((tm, d), lambda i: (i, 0)),
        out_shape=jax.ShapeDtypeStruct((ntok, d), F32),
        compiler_params=_cparams(("arbitrary",)),
        name="even_out",
    )(x, mod, o_f, o_b, z, o_c, gn_g.reshape(1, A_DV), w_out)


def _lin_out_kernel(x_ref, mod_ref, o_ref, w_ref, y_ref, *, gate_idx):
    g = mod_ref[0][gate_idx:gate_idx + 1]
    y_ref[...] = x_ref[...] + g * _bdot(o_ref[...], w_ref[...])


def _lin_out(x, mod, o, w, row_fn, tm, gate_idx):
    ntok, d = x.shape
    kdim = o.shape[1]
    return pl.pallas_call(
        functools.partial(_lin_out_kernel, gate_idx=gate_idx),
        grid=(ntok // tm,),
        in_specs=[pl.BlockSpec((tm, d), lambda i: (i, 0)),
                  pl.BlockSpec((1, N_MOD, d), lambda i: (row_fn(i), 0, 0)),
                  pl.BlockSpec((tm, kdim), lambda i: (i, 0)),
                  pl.BlockSpec((kdim, d), lambda i: (0, 0))],
        out_specs=pl.BlockSpec((tm, d), lambda i: (i, 0)),
        out_shape=jax.ShapeDtypeStruct((ntok, d), F32),
        compiler_params=_cparams(("arbitrary",)),
        name="lin_out",
    )(x, mod, o, w)


def _ctx_attn_kernel(q_ref, k_ref, v_ref, o_ref):
    scale = C_HD ** -0.5
    for h in range(C_HEADS):
        sl = slice(h * C_HD, (h + 1) * C_HD)
        s = _bdot_nt(q_ref[:, sl], k_ref[:, sl]) * scale
        e = jnp.exp(s - jnp.max(s, axis=-1, keepdims=True))
        p = e / jnp.sum(e, axis=-1, keepdims=True)
        o_ref[:, sl] = _bdot(p, v_ref[:, sl])


def _ctx_attn(zqkv, n_prompt, seq):
    ntok = zqkv.shape[0]
    w = C_HEADS * C_HD
    return pl.pallas_call(
        _ctx_attn_kernel,
        grid=(n_prompt // seq,),
        in_specs=[pl.BlockSpec((seq, w), lambda b: (b, 0)),
                  pl.BlockSpec((seq, w), lambda b: (b, 1)),
                  pl.BlockSpec((seq, w), lambda b: (b, 2))],
        out_specs=pl.BlockSpec((seq, w), lambda b: (b, 0)),
        out_shape=jax.ShapeDtypeStruct((ntok, w), F32),
        compiler_params=_cparams(("arbitrary",)),
        name="ctx_attn",
    )(zqkv, zqkv, zqkv)


NA_QROWS = 8
NA_KROWS = 16


def _na_blocks(rows):
    kh = min(WIN_H, rows)
    assert kh == WIN_H and rows % NA_QROWS == 0 and rows >= NA_KROWS
    blocks, types = [], []
    for m in range(rows // NA_QROWS):
        k0 = int(np.clip(NA_QROWS * m - kh // 2, 0, rows - NA_KROWS))
        tbl = np.full((NA_QROWS, NA_KROWS), 2 * WIN_H - 1, np.int64)
        for ql in range(NA_QROWS):
            rq = NA_QROWS * m + ql
            r0 = int(np.clip(rq - kh // 2, 0, rows - kh))
            for kl in range(NA_KROWS):
                rk = k0 + kl
                if r0 <= rk < r0 + kh:
                    tbl[ql, kl] = rk - rq + (WIN_H - 1)
        key = tbl.tobytes()
        if key not in [t.tobytes() for t in types]:
            types.append(tbl)
        blocks.append((NA_QROWS * m, k0, [t.tobytes() for t in types].index(key)))
    return blocks, types


def _na_kernel(o_in_ref, q_ref, k_ref, v_ref, kc_ref, vc_ref, m_ref, o_ref, bias_ref, *, blocks, types):
    del o_in_ref
    scale = C_HD ** -0.5

    @pl.when(pl.program_id(1) == 0)
    def _():
        for hh in range(LANES // C_HD):
            for t, tbl in enumerate(types):
                for ql in range(NA_QROWS):
                    for kl in range(0, NA_KROWS, 2):
                        pair = jnp.concatenate([m_ref[hh, int(tbl[ql, kl])], m_ref[hh, int(tbl[ql, kl + 1])]], axis=1)
                        bias_ref[hh, t, ql * GRID_W:(ql + 1) * GRID_W, kl * GRID_W:(kl + 2) * GRID_W] = pair

    for hh in range(LANES // C_HD):
        sl = slice(hh * C_HD, (hh + 1) * C_HD)
        kc = kc_ref[0, 0, hh]
        vc = vc_ref[0, 0, hh]
        for q0, k0, t in blocks:
            qs = slice(q0 * GRID_W, (q0 + NA_QROWS) * GRID_W)
            ks = slice(k0 * GRID_W, (k0 + NA_KROWS) * GRID_W)
            qb = q_ref[qs, sl]
            s_loc = _bdot_nt(qb, k_ref[ks, sl]) * scale + bias_ref[hh, t]
            s_ctx = _bdot_nt(qb, kc) * scale
            m = jnp.maximum(jnp.max(s_loc, axis=-1, keepdims=True), jnp.max(s_ctx, axis=-1, keepdims=True))
            e_loc = jnp.exp(s_loc - m)
            e_ctx = jnp.exp(s_ctx - m)
            den = jnp.sum(e_loc, axis=-1, keepdims=True) + jnp.sum(e_ctx, axis=-1, keepdims=True)
            o = _bdot(e_loc, v_ref[ks, sl]) + _bdot(e_ctx, vc)
            o_ref[qs, sl] = o / den


def _na_attn(o_prompt, zqkv, cache_k, cache_v, tiles, blocks, types, layer_j, n_prompt, dec_seq):
    ntok = zqkv.shape[0]
    nb = (ntok - n_prompt) // dec_seq
    boff = n_prompt // dec_seq
    hp = LANES // C_HD
    npair = C_HEADS // hp
    past = cache_k.shape[3]
    return pl.pallas_call(
        functools.partial(_na_kernel, blocks=blocks, types=types),
        grid=(npair, nb),
        in_specs=[pl.BlockSpec(memory_space=pl.ANY),
                  pl.BlockSpec((dec_seq, LANES), lambda p, b: (boff + b, p)),
                  pl.BlockSpec((dec_seq, LANES), lambda p, b: (boff + b, npair + p)),
                  pl.BlockSpec((dec_seq, LANES), lambda p, b: (boff + b, 2 * npair + p)),
                  pl.BlockSpec((1, 1, hp, past, C_HD), lambda p, b: (b, layer_j, p, 0, 0)),
                  pl.BlockSpec((1, 1, hp, past, C_HD), lambda p, b: (b, layer_j, p, 0, 0)),
                  pl.BlockSpec((hp,) + tiles.shape[1:], lambda p, b: (p, 0, 0, 0))],
        out_specs=pl.BlockSpec((dec_seq, LANES), lambda p, b: (boff + b, p)),
        out_shape=jax.ShapeDtypeStruct(o_prompt.shape, F32),
        scratch_shapes=[pltpu.VMEM((hp, len(types), NA_QROWS * GRID_W, NA_KROWS * GRID_W), F32)],
        input_output_aliases={0: 0},
        compiler_params=_cparams(("arbitrary", "arbitrary")),
        name="na_attn",
    )(o_prompt, zqkv, zqkv, zqkv, cache_k, cache_v, tiles)


def _na_bias_kernel(rpb_ref, e_ref, neg_ref, o_ref):
    h, m, l = _split3(rpb_ref[...])
    e = e_ref[...]
    d = lambda y: jnp.dot(y, e, preferred_element_type=F32)
    o_ref[...] = d(h) + (d(m) + d(l)) + neg_ref[...]


def _na_bias_tiles(rpb_l):
    nh, ndr, ndc = rpb_l.shape
    cq = np.arange(GRID_W)
    c0 = np.clip(cq - WIN_W // 2, 0, GRID_W - WIN_W)
    in_win = (cq[None, :] >= c0[:, None]) & (cq[None, :] < c0[:, None] + WIN_W)
    dc = np.clip(cq[None, :] - cq[:, None] + (WIN_W - 1), 0, ndc - 1)
    ndc_pad = 32
    onehot = (dc[None] == np.arange(ndc_pad)[:, None, None]) & in_win[None]
    onehot = jnp.asarray(onehot.reshape(ndc_pad, GRID_W * GRID_W), BF16)
    neg = jnp.asarray(np.where(in_win, 0.0, -np.inf).reshape(1, GRID_W * GRID_W), F32)
    rpb2 = jnp.pad(rpb_l.astype(F32).reshape(nh * ndr, ndc), ((0, 0), (0, ndc_pad - ndc)))
    m = pl.pallas_call(
        _na_bias_kernel,
        out_shape=jax.ShapeDtypeStruct((nh * ndr, GRID_W * GRID_W), F32),
        name="na_bias",
    )(rpb2, onehot, neg).reshape(nh, ndr, GRID_W, GRID_W)
    return jnp.concatenate([m, jnp.full((nh, 1, GRID_W, GRID_W), -jnp.inf, F32)], axis=1)


def _top_desc(x, n):
    vals = []
    for _ in range(n):
        m = jnp.max(x, axis=0, keepdims=True)
        vals.append(m)
        x = jnp.where(x == m, -jnp.inf, x)
    return vals


SUBLANES = 8


def _sort_network(n):
    pairs = []
    p = 1
    while p < n:
        k = p
        while k >= 1:
            for j in range(k % p, n - k, 2 * k):
                for i in range(min(k, n - j - k)):
                    if (i + j) // (2 * p) == (i + j + k) // (2 * p):
                        pairs.append((i + j, i + j + k))
            k //= 2
        p *= 2
    return pairs


def _top_rows(x, n):
    g = [x[SUBLANES * k:SUBLANES * (k + 1)] for k in range(n)]
    for i, j in _sort_network(n):
        g[i], g[j] = jnp.maximum(g[i], g[j]), jnp.minimum(g[i], g[j])
    shift = SUBLANES // 2
    while shift >= 1:
        other = [pltpu.roll(a, shift, axis=0) for a in g]
        g = [jnp.maximum(g[k], other[n - 1 - k]) for k in range(n)]
        stride = n // 2
        while stride >= 1:
            for k in range(n):
                if k & stride == 0:
                    g[k], g[k + stride] = jnp.maximum(g[k], g[k + stride]), jnp.minimum(g[k], g[k + stride])
            stride //= 2
        shift //= 2
    tops = [a[0:1] for a in g]
    nxt = jnp.max(jnp.where(x < tops[n - 1], x, -jnp.inf), axis=0, keepdims=True)
    return tops + [nxt]


def _peer_gate_kernel(x_ref, g_ref, mod_ref, wh_ref, wl_ref, keys_ref, a_ref, b_ref, tau_ref, hn_ref, hq_ref, *,
                      shift_idx, scale_idx):
    kq = PEER_TOPK
    m = mod_ref[0]
    hn = _modnorm(x_ref[...], g_ref[...], m[shift_idx:shift_idx + 1], m[scale_idx:scale_idx + 1])
    hn_hi, hn_lo = _split2(hn)
    hn_ref[0] = hn_hi
    hn_ref[1] = hn_lo
    d = lambda x, y: jnp.dot(x, y, preferred_element_type=F32)

    def query(h):
        cols = slice(h * 2 * N_KEYS, (h + 1) * 2 * N_KEYS)
        wh = wh_ref[:, cols]
        hq_ref[h % 2] = d(hn_ref[0], wh) + (d(hn_ref[0], wl_ref[:, cols]) + d(hn_ref[1], wh))

    query(0)
    for h in range(PEER_HEADS):
        if h + 1 < PEER_HEADS:
            query(h + 1)
        hq = hq_ref[h % 2]
        s = []
        tops = []
        for p in range(2):
            sp = _dot3_nt(keys_ref[h, p], hq[:, p * N_KEYS:(p + 1) * N_KEYS])
            s.append(sp)
            tops.append(_top_rows(sp, kq))
        t0 = jnp.concatenate(tops[0][:kq], axis=0)
        t1 = jnp.concatenate(tops[1][:kq], axis=0)
        ninf = jnp.full((4, t0.shape[1]), -jnp.inf, F32)
        cand = jnp.concatenate([
            tops[0][0] + t1,
            tops[0][1] + t1[0:8],
            tops[0][2] + t1[0:8],
            tops[0][3] + t1[0:8],
            t0[8:16] + tops[1][0],
            t0[4:8] + tops[1][0], t0[4:8] + tops[1][1],
            t0[4:8] + tops[1][2], ninf], axis=0)
        best = _top_desc(cand, kq + 1)
        mx = best[0]
        z = jnp.zeros_like(mx)
        for r in range(kq):
            z = z + jnp.exp(best[r] - mx)
        nxt = jnp.maximum(best[kq], jnp.maximum(tops[0][kq] + tops[1][0], tops[0][0] + tops[1][kq]))
        thr = 0.5 * (best[kq - 1] + nxt)
        inv_z = 1.0 / z
        a_ref[h] = jnp.exp(s[0] - tops[0][0]) * inv_z
        b_ref[h] = jnp.exp(s[1] - tops[1][0]).astype(BF16)
        tau_ref[h:h + 1, :] = jnp.exp(thr - mx) * inv_z


def _peer_gates(x, g, mod, wq, keys, row_fn, tmk, shift_idx, scale_idx):
    ntok, d = x.shape
    nq = wq.shape[1]
    wh = wq.astype(BF16)
    wl = (wq - wh.astype(F32)).astype(BF16)
    return pl.pallas_call(
        functools.partial(_peer_gate_kernel, shift_idx=shift_idx, scale_idx=scale_idx),
        grid=(ntok // tmk,),
        in_specs=[pl.BlockSpec((tmk, d), lambda i: (i, 0)),
                  pl.BlockSpec((1, d), lambda i: (0, 0)),
                  pl.BlockSpec((1, N_MOD, d), lambda i: (row_fn(i), 0, 0)),
                  pl.BlockSpec((d, nq), lambda i: (0, 0)),
                  pl.BlockSpec((d, nq), lambda i: (0, 0)),
                  pl.BlockSpec(keys.shape, lambda i: (0, 0, 0, 0))],
        out_specs=[pl.BlockSpec((PEER_HEADS, N_KEYS, tmk), lambda i: (0, 0, i)),
                   pl.BlockSpec((PEER_HEADS, N_KEYS, tmk), lambda i: (0, 0, i)),
                   pl.BlockSpec((PEER_HEADS, tmk), lambda i: (0, i))],
        out_shape=[jax.ShapeDtypeStruct((PEER_HEADS, N_KEYS, ntok), F32),
                   jax.ShapeDtypeStruct((PEER_HEADS, N_KEYS, ntok), BF16),
                   jax.ShapeDtypeStruct((PEER_HEADS, ntok), F32)],
        scratch_shapes=[pltpu.VMEM((2, tmk, d), BF16),
                        pltpu.VMEM((2, tmk, 2 * N_KEYS), F32)],
        compiler_params=_cparams(("arbitrary",)),
        name="peer_gates",
    )(x, g.reshape(1, d), mod, wh, wl, keys)


BF16_ROWS = 16
PEER_TOK_SUB = 256


def _peer_mix_kernel(x_ref, g_ref, mod_ref, u_ref, vt_ref, a_ref, b_ref, tau_ref, y_ref, h_ref, acc_ref, act_ref,
                     ht_ref, *, shift_idx, scale_idx, gate_idx):
    e = pl.program_id(1)
    te = u_ref.shape[0]
    tm = x_ref.shape[0]
    nsub = te // N_KEYS
    nq = tm // PEER_TOK_SUB

    @pl.when(e == 0)
    def _():
        m = mod_ref[0]
        h = _modnorm(x_ref[...], g_ref[...], m[shift_idx:shift_idx + 1], m[scale_idx:scale_idx + 1])
        h_ref[...] = h.T.astype(BF16)
        acc_ref[...] = jnp.zeros_like(acc_ref)

    def activations(q):
        cs = slice(q * PEER_TOK_SUB, (q + 1) * PEER_TOK_SUB)
        act_ref[q % 2] = jnp.dot(u_ref[...], h_ref[:, cs], preferred_element_type=F32)

    activations(0)
    for q in range(nq):
        cs = slice(q * PEER_TOK_SUB, (q + 1) * PEER_TOK_SUB)
        if q + 1 < nq:
            activations(q + 1)
        grp = (N_KEYS // BF16_ROWS, BF16_ROWS, LANES)
        bcast = lambda r: jnp.broadcast_to(r, (BF16_ROWS, LANES)).astype(BF16)[None]
        zero = jnp.zeros(grp, BF16)
        nlb = PEER_TOK_SUB // LANES
        lanes = [slice(lb * LANES, (lb + 1) * LANES) for lb in range(nlb)]
        taus = [[bcast(tau_ref[h:h + 1, cs][:, sub]) for h in range(PEER_HEADS)] for sub in lanes]
        for ii in range(nsub):
            i = e * nsub + ii
            rows = slice(ii * N_KEYS, (ii + 1) * N_KEYS)
            arows = [a_ref[h, pl.ds(i, 1), cs] for h in range(PEER_HEADS)]
            for lb, sub in enumerate(lanes):
                cl = slice(q * PEER_TOK_SUB + lb * LANES, q * PEER_TOK_SUB + (lb + 1) * LANES)
                gsum = None
                for h in range(PEER_HEADS):
                    w = bcast(arows[h][:, sub]) * b_ref[h, :, cl].reshape(grp)
                    term = jnp.where(w >= taus[lb][h], w, zero)
                    gsum = term if gsum is None else gsum + term
                act = jax.nn.gelu(act_ref[q % 2, rows, sub].astype(BF16)).reshape(grp)
                ht_ref[q % 2, rows, sub] = (act * gsum).reshape(N_KEYS, LANES)
        acc_ref[:, cs] += jnp.dot(vt_ref[...], ht_ref[q % 2], preferred_element_type=F32)

    @pl.when(e == pl.num_programs(1) - 1)
    def _():
        g = mod_ref[0][gate_idx:gate_idx + 1]
        y_ref[...] = x_ref[...] + g * acc_ref[...].T


def _peer_mix(x, g, mod, u_bf, vt_bf, a, b, tau, row_fn, tm, te, shift_idx, scale_idx, gate_idx):
    ntok, d = x.shape
    nexp = u_bf.shape[0]
    return pl.pallas_call(
        functools.partial(_peer_mix_kernel, shift_idx=shift_idx, scale_idx=scale_idx, gate_idx=gate_idx),
        grid=(ntok // tm, nexp // te),
        in_specs=[pl.BlockSpec((tm, d), lambda t, e: (t, 0)),
                  pl.BlockSpec((1, d), lambda t, e: (0, 0)),
                  pl.BlockSpec((1, N_MOD, d), lambda t, e: (row_fn(t), 0, 0)),
                  pl.BlockSpec((te, d), lambda t, e: (e, 0)),
                  pl.BlockSpec((d, te), lambda t, e: (0, e)),
                  pl.BlockSpec((PEER_HEADS, N_KEYS, tm), lambda t, e: (0, 0, t)),
                  pl.BlockSpec((PEER_HEADS, N_KEYS, tm), lambda t, e: (0, 0, t)),
                  pl.BlockSpec((PEER_HEADS, tm), lambda t, e: (0, t))],
        out_specs=pl.BlockSpec((tm, d), lambda t, e: (t, 0)),
        out_shape=jax.ShapeDtypeStruct((ntok, d), F32),
        scratch_shapes=[pltpu.VMEM((d, tm), BF16),
                        pltpu.VMEM((d, tm), F32),
                        pltpu.VMEM((2, te, PEER_TOK_SUB), F32),
                        pltpu.VMEM((2, te, PEER_TOK_SUB), BF16)],
        compiler_params=_cparams(("arbitrary", "arbitrary")),
        name="peer_mix",
    )(x, g.reshape(1, d), mod, u_bf, vt_bf, a, b, tau)


def _seq_tables(n_prompt, seq, n_dec, dec_seq, tb):
    seq_of, first, last, cached, has = [], [], [], [], []
    sid = 0
    for count, length, is_dec in ((n_prompt // seq, seq, 0), (n_dec, dec_seq, 1)):
        nb = length // tb
        for k in range(count):
            for b in range(nb):
                seq_of.append(sid)
                first.append(int(b == 0))
                last.append(int(b == nb - 1))
                cached.append(k if is_dec else 0)
                has.append(is_dec)
            sid += 1
    nblk = len(seq_of)
    fwd = np.array([list(range(nblk)), seq_of, first, cached, has], np.int32)
    order = list(range(nblk - 1, -1, -1))
    pick = lambda a: [a[i] for i in order]
    bwd = np.array([order, pick(seq_of), pick(last), pick(cached), pick(has)], np.int32)
    halo = np.array([[1 - f for f in first], [1 - l for l in last]], np.int32)
    return fwd, bwd, halo, sid


def kernel(x_prompt, x_sample, cache_k, cache_v, state_fwd, state_bwd, c, c_ctx, norm_g, w_mod, b_mod, w_in_ab, hg_lb,
           hg_norm_g, cv_w, cv_b, cv_ln_g, cv_ln_b, w_out_ab, w_qkv, qk_g, rpb, w_out_na, peer_wq, peer_keys, peer_u,
           peer_v):
    bp, seq, d = x_prompt.shape
    nd, dec_seq, _ = x_sample.shape
    depth = w_mod.shape[0]
    n_prompt = bp * seq
    ntok = n_prompt + nd * dec_seq
    assert n_prompt % dec_seq == 0 and seq % TOK_BLK == 0 and dec_seq % TOK_BLK == 0
    rows = dec_seq // GRID_W
    n_qk = A_HEADS * A_DK
    n_v = A_HEADS * A_DV
    n_b = d - n_v

    tile = lambda limit: max(t for t in (128, 256, 512, 1024) if t <= limit and n_prompt % t == 0 and dec_seq % t == 0)
    tm_lin = tile(1024)
    tm_out = tile(512)
    tm_peer = tile(1024)
    te_peer = 512
    tmk = 256

    x = jnp.concatenate([x_prompt.reshape(n_prompt, d), x_sample.reshape(nd * dec_seq, d)], axis=0)
    cond8 = jnp.zeros((8, d), F32).at[0].set(c_ctx).at[1:1 + nd].set(c)
    mods = _modulation(cond8, w_mod, b_mod).reshape(depth, 8, N_MOD, d)

    fwd_tbl, bwd_tbl, halo_tbl, nseq = _seq_tables(n_prompt, seq, nd, dec_seq, TOK_BLK)
    fwd_tbl, bwd_tbl, halo_tbl = jnp.asarray(fwd_tbl), jnp.asarray(bwd_tbl), jnp.asarray(halo_tbl)
    hg_lb_t = jnp.transpose(hg_lb, (1, 0, 2))

    row = lambda tm: _mod_row(n_prompt, dec_seq, tm)
    tn_qkv = 512
    bd = (jnp.arange(tn_qkv)[:, None] // C_HD == jnp.arange(tn_qkv)[None, :] // C_HD).astype(BF16) * (1.0 / C_HD)

    new_k, new_v, new_sf, new_sb = [], [], [], []
    for l in range(depth):
        j = l // 2
        mod = mods[l]
        if l % 2 == 0:
            z = _normlin(x, norm_g[l, 0], mod, w_in_ab[j], row(tm_lin), tm_lin, 512, 0, 1)
            blk = lambda col: col // A_DK
            o_f, sf = _gla(z, hg_lb_t, state_fwd, fwd_tbl, nseq, reverse=False, layer_j=j,
                           qcol=0, fcol=blk(n_qk), vcol=blk(3 * n_qk))
            o_b, sb = _gla(z, hg_lb_t, state_bwd, bwd_tbl, nseq, reverse=True, layer_j=j,
                           qcol=0, fcol=blk(2 * n_qk), vcol=blk(3 * n_qk))
            new_sf.append(sf[:bp])
            new_sb.append(sb[:bp])
            cbase = 3 * n_qk + 2 * n_v
            o_c = _conv(z, halo_tbl, cv_w[j], cv_b[j], cv_ln_g[j], cv_ln_b[j],
                        acol=cbase // n_b, gcol=(cbase + n_b) // n_b)
            x = _even_out(x, mod, o_f, o_b, z, o_c, hg_norm_g[j], w_out_ab[j], row(tm_out), tm_out, 2,
                          (3 * n_qk + n_v) // n_v)
        else:
            wq = C_HEADS * C_HD
            qg = jnp.concatenate([jnp.tile(qk_g[j, 0], C_HEADS), jnp.tile(qk_g[j, 1], C_HEADS)]).reshape(1, 2 * wq)
            z = _normlin(x, norm_g[l, 0], mod, w_qkv[j], row(tm_lin), tm_lin, tn_qkv, 0, 1, qk=(bd, qg, 2 * wq))
            kv = z[:n_prompt].reshape(bp, seq, 3, C_HEADS, C_HD)
            new_k.append(jnp.transpose(kv[:, :, 1], (0, 2, 1, 3)))
            new_v.append(jnp.transpose(kv[:, :, 2], (0, 2, 1, 3)))
            o = _ctx_attn(z, n_prompt, seq)
            na_blocks, na_types = _na_blocks(rows)
            o = _na_attn(o, z, cache_k, cache_v, _na_bias_tiles(rpb[j]), na_blocks, na_types, j, n_prompt, dec_seq)
            x = _lin_out(x, mod, o, w_out_na[j], row(tm_out), tm_out, 2)
        a, b, tau = _peer_gates(x, norm_g[l, 1], mod, peer_wq[l], peer_keys[l], row(tmk), tmk, 3, 4)
        x = _peer_mix(x, norm_g[l, 1], mod, peer_u[l].astype(BF16), peer_v[l].T.astype(BF16), a, b, tau,
                      row(tm_peer), tm_peer, te_peer, 3, 4, 5)

    yp = x[:n_prompt].reshape(bp, seq, d)
    ys = x[n_prompt:].reshape(nd, dec_seq, d)
    return (yp, ys, jnp.stack(new_k, axis=1), jnp.stack(new_v, axis=1),
            jnp.stack(new_sf, axis=1), jnp.stack(new_sb, axis=1))
```

```python
import functools

import numpy as np
import jax
import jax.numpy as jnp
from jax import lax
from jax.experimental import pallas as pl
from jax.experimental.pallas import tpu as pltpu

F32 = jnp.float32
BF16 = jnp.bfloat16

GRID_W = 64
A_HEADS = 4
A_DK = 128
A_DV = 128
CONV_W = 31
GLA_CHUNK = 16
C_HEADS = 16
C_HD = 64
WIN_H = 8
WIN_W = 16
PEER_HEADS = 8
N_KEYS = 128
PEER_TOPK = 16
N_MOD = 6
EPS = 1e-6

LANES = 128
TOK_BLK = 256
CONV_HALO = 16
VMEM_LIMIT = 56 * 1024 * 1024


def _cparams(sem):
    return pltpu.CompilerParams(dimension_semantics=sem, vmem_limit_bytes=VMEM_LIMIT)


def _bdot(a, b):
    return jnp.dot(a.astype(BF16), b.astype(BF16), preferred_element_type=F32)


def _bdot_nt(a, b):
    return lax.dot_general(a.astype(BF16), b.astype(BF16), (((1,), (1,)), ((), ())), preferred_element_type=F32)


def _split2(a):
    hi = a.astype(BF16)
    lo = (a - hi.astype(F32)).astype(BF16)
    return hi, lo


def _split3(a):
    hi = a.astype(BF16)
    r = a - hi.astype(F32)
    mid = r.astype(BF16)
    lo = (r - mid.astype(F32)).astype(BF16)
    return hi, mid, lo


def _dot3(a, b):
    ah, al = _split2(a)
    bh, bl = _split2(b)
    d = lambda x, y: jnp.dot(x, y, preferred_element_type=F32)
    return d(ah, bh) + (d(ah, bl) + d(al, bh))


def _dot3_nt(a, b):
    ah, al = _split2(a)
    bh, bl = _split2(b)
    d = lambda x, y: lax.dot_general(x, y, (((1,), (1,)), ((), ())), preferred_element_type=F32)
    return d(ah, bh) + (d(ah, bl) + d(al, bh))


def _exact_left(mask_bf16, x):
    h, m, l = _split3(x)
    d = lambda y: jnp.dot(mask_bf16, y, preferred_element_type=F32)
    return d(h) + (d(m) + d(l))


def _mod_row(n_prompt, dec_seq, tm):
    def row(i):
        tok = i * tm
        return jnp.where(tok < n_prompt, 0, 1 + (tok - n_prompt) // dec_seq)
    return row


def _modnorm(x, g, shift, scale):
    ms = jnp.mean(x * x, axis=-1, keepdims=True)
    return x * lax.rsqrt(ms + EPS) * g * (1.0 + scale) + shift


def _mod_kernel(c_ref, w_ref, b_ref, o_ref):
    s = jax.nn.silu(c_ref[...])
    o_ref[0] = _dot3(s, w_ref[0]) + b_ref[0]


def _modulation(cond8, w_mod, b_mod):
    depth, d, n = w_mod.shape
    tn = 1536
    return pl.pallas_call(
        _mod_kernel,
        grid=(depth, n // tn),
        in_specs=[pl.BlockSpec((8, d), lambda l, j: (0, 0)),
                  pl.BlockSpec((1, d, tn), lambda l, j: (l, 0, j)),
                  pl.BlockSpec((1, 1, tn), lambda l, j: (l, 0, j))],
        out_specs=pl.BlockSpec((1, 8, tn), lambda l, j: (l, 0, j)),
        out_shape=jax.ShapeDtypeStruct((depth, 8, n), F32),
        compiler_params=_cparams(("arbitrary", "arbitrary")),
        name="modulation",
    )(cond8, w_mod, b_mod.reshape(depth, 1, n))


def _normlin_kernel(x_ref, g_ref, mod_ref, w_ref, *rest, shift_idx, scale_idx, qk_tiles):
    if qk_tiles:
        bd_ref, qg_ref, o_ref, h_ref = rest
    else:
        o_ref, h_ref = rest
    j = pl.program_id(1)

    @pl.when(j == 0)
    def _():
        m = mod_ref[0]
        h = _modnorm(x_ref[...], g_ref[...], m[shift_idx:shift_idx + 1], m[scale_idx:scale_idx + 1])
        h_ref[...] = h.astype(BF16)

    acc = jnp.dot(h_ref[...], w_ref[...].astype(BF16), preferred_element_type=F32)

    if qk_tiles:
        @pl.when(j < qk_tiles)
        def _():
            sh, sl = _split2(acc * acc)
            bd = bd_ref[...]
            ms = jnp.dot(sh, bd, preferred_element_type=F32) + jnp.dot(sl, bd, preferred_element_type=F32)
            o_ref[...] = acc * lax.rsqrt(ms + EPS) * qg_ref[...]

        @pl.when(j >= qk_tiles)
        def _():
            o_ref[...] = acc
    else:
        o_ref[...] = acc


def _normlin(x, g, mod, w, row_fn, tm, tn, shift_idx, scale_idx, qk=None):
    ntok, d = x.shape
    n = w.shape[1]
    in_specs = [pl.BlockSpec((tm, d), lambda i, j: (i, 0)),
                pl.BlockSpec((1, d), lambda i, j: (0, 0)),
                pl.BlockSpec((1, N_MOD, d), lambda i, j: (row_fn(i), 0, 0)),
                pl.BlockSpec((d, tn), lambda i, j: (0, j))]
    args = [x, g.reshape(1, d), mod, w]
    qk_tiles = 0
    if qk is not None:
        bd, qg, n_qk = qk
        qk_tiles = n_qk // tn
        in_specs += [pl.BlockSpec((tn, tn), lambda i, j: (0, 0)),
                     pl.BlockSpec((1, tn), lambda i, j: (0, jnp.minimum(j, qk_tiles - 1)))]
        args += [bd, qg]
    return pl.pallas_call(
        functools.partial(_normlin_kernel, shift_idx=shift_idx, scale_idx=scale_idx, qk_tiles=qk_tiles),
        grid=(ntok // tm, n // tn),
        in_specs=in_specs,
        out_specs=pl.BlockSpec((tm, tn), lambda i, j: (i, j)),
        out_shape=jax.ShapeDtypeStruct((ntok, n), F32),
        scratch_shapes=[pltpu.VMEM((tm, d), BF16)],
        compiler_params=_cparams(("arbitrary", "arbitrary")),
        name="normlin",
    )(*args)


GLA_HEADS_PER_STEP = 2


def _gla_kernel(tbl_ref, zq_ref, zf_ref, zv_ref, lb_ref, s0_ref, o_ref, sfin_ref, st_ref, *, reverse, layer_j):
    step = pl.program_id(1)
    tb = zq_ref.shape[0]
    nchunk = tb // GLA_CHUNK
    nh = zq_ref.shape[1] // A_DK

    @pl.when((tbl_ref[2, step] == 1) & (tbl_ref[4, step] == 1))
    def _():
        for hh in range(nh):
            st_ref[hh] = s0_ref[0, 0, hh].T

    @pl.when((tbl_ref[2, step] == 1) & (tbl_ref[4, step] == 0))
    def _():
        st_ref[...] = jnp.zeros_like(st_ref)

    row = lax.broadcasted_iota(jnp.int32, (tb, tb), 0)
    col = lax.broadcasted_iota(jnp.int32, (tb, tb), 1)
    same = (row // GLA_CHUNK) == (col // GLA_CHUNK)
    if reverse:
        incl = same & (col >= row)
        rest = same & (col < row)
    else:
        incl = same & (col <= row)
        rest = same & (col > row)
    incl_m = jnp.where(incl, 1.0, 0.0).astype(BF16)
    rest_m = jnp.where(rest, 1.0, 0.0).astype(BF16)
    ridx = lax.broadcasted_iota(jnp.int32, (tb, A_DK), 0) // GLA_CHUNK
    cidx_l = lax.broadcasted_iota(jnp.int32, (A_DV, tb), 1) // GLA_CHUNK

    lbx = lb_ref[0]
    e = jnp.exp(lbx - jnp.max(lbx, axis=0, keepdims=True))
    p = e / jnp.sum(e, axis=0, keepdims=True)
    lb_all = jnp.zeros((1, lbx.shape[1]), F32)
    for i in range(1, layer_j + 1):
        lb_all = lb_all + p[i:i + 1]

    for hh in range(nh):
        sl = slice(hh * A_DK, (hh + 1) * A_DK)
        lb = lb_all[:, sl]
        z = zf_ref[:, sl]
        log_sig = jnp.minimum(z, 0.0) - jnp.log1p(jnp.exp(-jnp.abs(z)))
        a_ = jnp.log1p(-lb) + log_sig
        b_ = jnp.log(lb)
        logf = jnp.maximum(a_, b_) + jnp.log1p(jnp.exp(-jnp.abs(a_ - b_)))
        k = (1.0 - lb) * jax.nn.sigmoid(-z)
        q = zq_ref[:, sl] * (A_DK ** -0.5)
        v = zv_ref[:, sl]

        bc = _exact_left(incl_m, logf)
        brem = _exact_left(rest_m, logf)
        qd = q * jnp.exp(bc)
        kinv = k * jnp.exp(-bc)
        att = jnp.where(incl, _bdot_nt(qd, kinv), 0.0)
        o_intra = _bdot(att, v)
        klast = k * jnp.exp(brem)
        vt = v.T
        order = list(range(nchunk - 1, -1, -1) if reverse else range(nchunk))
        vt_blocks = jnp.concatenate([jnp.where(cidx_l == c, vt, 0.0) for c in order], axis=0)
        upd = _bdot(vt_blocks, klast)
        st = st_ref[hh]
        starts = []
        for pos, c in enumerate(order):
            starts.append(st)
            last = c * GLA_CHUNK if reverse else (c + 1) * GLA_CHUNK - 1
            a_c = jnp.exp(bc[last:last + 1])
            st = a_c * st + upd[pos * A_DV:(pos + 1) * A_DV]
        qd_blocks = jnp.concatenate([jnp.where(ridx == c, qd, 0.0) for c in order], axis=1)
        o_inter = _bdot_nt(qd_blocks, jnp.concatenate(starts, axis=1))
        st_ref[hh] = st
        o_ref[:, sl] = o_intra + o_inter
        sfin_ref[0, hh] = st.T


def _gla(z, hg_lb_t, s0, tbl, nseq, *, reverse, layer_j, qcol, fcol, vcol):
    ntok = z.shape[0]
    nsteps = tbl.shape[1]
    tb = TOK_BLK
    n_rec = hg_lb_t.shape[1]
    d = 1 if reverse else 0
    nh = GLA_HEADS_PER_STEP
    w = nh * A_DK
    qb, fb, vb = qcol * A_DK // w, fcol * A_DK // w, vcol * A_DK // w
    grid_spec = pltpu.PrefetchScalarGridSpec(
        num_scalar_prefetch=1,
        grid=(A_HEADS // nh, nsteps),
        in_specs=[pl.BlockSpec((tb, w), lambda h, s, t: (t[0, s], qb + h)),
                  pl.BlockSpec((tb, w), lambda h, s, t: (t[0, s], fb + h)),
                  pl.BlockSpec((tb, w), lambda h, s, t: (t[0, s], vb + h)),
                  pl.BlockSpec((1, n_rec, w), lambda h, s, t: (d, 0, h)),
                  pl.BlockSpec((1, 1, nh, A_DK, A_DV), lambda h, s, t: (t[3, s], layer_j, h, 0, 0))],
        out_specs=[pl.BlockSpec((tb, w), lambda h, s, t: (t[0, s], h)),
                   pl.BlockSpec((1, nh, A_DK, A_DV), lambda h, s, t: (t[1, s], h, 0, 0))],
        scratch_shapes=[pltpu.VMEM((nh, A_DV, A_DK), F32)])
    return pl.pallas_call(
        functools.partial(_gla_kernel, reverse=reverse, layer_j=layer_j),
        grid_spec=grid_spec,
        out_shape=[jax.ShapeDtypeStruct((ntok, A_HEADS * A_DV), F32),
                   jax.ShapeDtypeStruct((nseq, A_HEADS, A_DK, A_DV), F32)],
        compiler_params=_cparams(("arbitrary", "arbitrary")),
        name="gla_bwd" if reverse else "gla_fwd",
    )(tbl, z, z, z, hg_lb_t, s0)


def _conv_kernel(tbl_ref, ua_ref, ug_ref, pa_ref, pg_ref, na_ref, ng_ref, w_ref, b_ref, lg_ref, lbias_ref, o_ref,
                 buf_ref):
    i = pl.program_id(0)
    tb = ua_ref.shape[0]
    hl = CONV_HALO
    prev_ok = tbl_ref[0, i] == 1
    next_ok = tbl_ref[1, i] == 1
    buf_ref[hl:hl + tb, :] = ua_ref[...] * jax.nn.sigmoid(ug_ref[...])
    buf_ref[0:hl, :] = jnp.where(prev_ok, pa_ref[...] * jax.nn.sigmoid(pg_ref[...]), 0.0)
    buf_ref[hl + tb:hl + tb + hl, :] = jnp.where(next_ok, na_ref[...] * jax.nn.sigmoid(ng_ref[...]), 0.0)
    w = w_ref[...]
    acc = jnp.zeros(o_ref.shape, F32) + b_ref[...]
    base = hl - CONV_W // 2
    for t in range(CONV_W):
        acc = acc + buf_ref[base + t:base + t + tb, :] * w[t:t + 1]
    mu = jnp.mean(acc, axis=-1, keepdims=True)
    xc = acc - mu
    var = jnp.mean(xc * xc, axis=-1, keepdims=True)
    y = xc * lax.rsqrt(var + EPS) * lg_ref[...] + lbias_ref[...]
    o_ref[...] = jax.nn.silu(y)


def _conv(z, tbl, cv_w, cv_b, ln_g, ln_b, *, acol, gcol):
    ntok = z.shape[0]
    ch = cv_w.shape[1]
    tb = TOK_BLK
    hl = CONV_HALO
    r = tb // hl
    last = ntok // hl - 1
    prev = lambda i, t: jnp.maximum(i * r - 1, 0)
    nxt = lambda i, t: jnp.minimum((i + 1) * r, last)
    grid_spec = pltpu.PrefetchScalarGridSpec(
        num_scalar_prefetch=1,
        grid=(ntok // tb,),
        in_specs=[pl.BlockSpec((tb, ch), lambda i, t: (i, acol)),
                  pl.BlockSpec((tb, ch), lambda i, t: (i, gcol)),
                  pl.BlockSpec((hl, ch), lambda i, t: (prev(i, t), acol)),
                  pl.BlockSpec((hl, ch), lambda i, t: (prev(i, t), gcol)),
                  pl.BlockSpec((hl, ch), lambda i, t: (nxt(i, t), acol)),
                  pl.BlockSpec((hl, ch), lambda i, t: (nxt(i, t), gcol)),
                  pl.BlockSpec((CONV_W, ch), lambda i, t: (0, 0)),
                  pl.BlockSpec((1, ch), lambda i, t: (0, 0)),
                  pl.BlockSpec((1, ch), lambda i, t: (0, 0)),
                  pl.BlockSpec((1, ch), lambda i, t: (0, 0))],
        out_specs=pl.BlockSpec((tb, ch), lambda i, t: (i, 0)),
        scratch_shapes=[pltpu.VMEM((tb + 2 * hl, ch), F32)])
    return pl.pallas_call(
        _conv_kernel,
        grid_spec=grid_spec,
        out_shape=jax.ShapeDtypeStruct((ntok, ch), F32),
        compiler_params=_cparams(("arbitrary",)),
        name="conv_module",
    )(tbl, z, z, z, z, z, z, cv_w, cv_b.reshape(1, ch), ln_g.reshape(1, ch), ln_b.reshape(1, ch))


def _even_out_kernel(x_ref, mod_ref, of_ref, ob_ref, zg_ref, oc_ref, gn_ref, w_ref, y_ref, *, gate_idx):
    o = of_ref[...] + ob_ref[...]
    gate = jax.nn.silu(zg_ref[...])
    gn = gn_ref[...]
    parts = []
    for h in range(A_HEADS):
        oh = o[:, h * A_DV:(h + 1) * A_DV]
        ms = jnp.mean(oh * oh, axis=-1, keepdims=True)
        parts.append(oh * lax.rsqrt(ms + EPS) * gn * gate[:, h * A_DV:(h + 1) * A_DV])
    o_a = jnp.concatenate(parts, axis=-1)
    na = o_a.shape[1]
    acc = _bdot(o_a, w_ref[0:na, :]) + _bdot(oc_ref[...], w_ref[na:, :])
    g = mod_ref[0][gate_idx:gate_idx + 1]
    y_ref[...] = x_ref[...] + g * acc


def _even_out(x, mod, o_f, o_b, z, o_c, gn_g, w_out, row_fn, tm, gate_idx, gcol):
    ntok, d = x.shape
    na = o_f.shape[1]
    nc = o_c.shape[1]
    return pl.pallas_call(
        functools.partial(_even_out_kernel, gate_idx=gate_idx),
        grid=(ntok // tm,),
        in_specs=[pl.BlockSpec((tm, d), lambda i: (i, 0)),
                  pl.BlockSpec((1, N_MOD, d), lambda i: (row_fn(i), 0, 0)),
                  pl.BlockSpec((tm, na), lambda i: (i, 0)),
                  pl.BlockSpec((tm, na), lambda i: (i, 0)),
                  pl.BlockSpec((tm, na), lambda i: (i, gcol)),
                  pl.BlockSpec((tm, nc), lambda i: (i, 0)),
                  pl.BlockSpec((1, A_DV), lambda i: (0, 0)),
                  pl.BlockSpec((na + nc, d), lambda i: (0, 0))],
        out_specs=pl.BlockSpec((tm, d), lambda i: (i, 0)),
        out_shape=jax.ShapeDtypeStruct((ntok, d), F32),
        compiler_params=_cparams(("arbitrary",)),
        name="even_out",
    )(x, mod, o_f, o_b, z, o_c, gn_g.reshape(1, A_DV), w_out)


def _lin_out_kernel(x_ref, mod_ref, o_ref, w_ref, y_ref, *, gate_idx):
    g = mod_ref[0][gate_idx:gate_idx + 1]
    y_ref[...] = x_ref[...] + g * _bdot(o_ref[...], w_ref[...])


def _lin_out(x, mod, o, w, row_fn, tm, gate_idx):
    ntok, d = x.shape
    kdim = o.shape[1]
    return pl.pallas_call(
        functools.partial(_lin_out_kernel, gate_idx=gate_idx),
        grid=(ntok // tm,),
        in_specs=[pl.BlockSpec((tm, d), lambda i: (i, 0)),
                  pl.BlockSpec((1, N_MOD, d), lambda i: (row_fn(i), 0, 0)),
                  pl.BlockSpec((tm, kdim), lambda i: (i, 0)),
                  pl.BlockSpec((kdim, d), lambda i: (0, 0))],
        out_specs=pl.BlockSpec((tm, d), lambda i: (i, 0)),
        out_shape=jax.ShapeDtypeStruct((ntok, d), F32),
        compiler_params=_cparams(("arbitrary",)),
        name="lin_out",
    )(x, mod, o, w)


def _ctx_attn_kernel(q_ref, k_ref, v_ref, o_ref):
    scale = C_HD ** -0.5
    for h in range(C_HEADS):
        sl = slice(h * C_HD, (h + 1) * C_HD)
        s = _bdot_nt(q_ref[:, sl], k_ref[:, sl]) * scale
        e = jnp.exp(s - jnp.max(s, axis=-1, keepdims=True))
        p = e / jnp.sum(e, axis=-1, keepdims=True)
        o_ref[:, sl] = _bdot(p, v_ref[:, sl])


def _ctx_attn(zqkv, n_prompt, seq):
    ntok = zqkv.shape[0]
    w = C_HEADS * C_HD
    return pl.pallas_call(
        _ctx_attn_kernel,
        grid=(n_prompt // seq,),
        in_specs=[pl.BlockSpec((seq, w), lambda b: (b, 0)),
                  pl.BlockSpec((seq, w), lambda b: (b, 1)),
                  pl.BlockSpec((seq, w), lambda b: (b, 2))],
        out_specs=pl.BlockSpec((seq, w), lambda b: (b, 0)),
        out_shape=jax.ShapeDtypeStruct((ntok, w), F32),
        compiler_params=_cparams(("arbitrary",)),
        name="ctx_attn",
    )(zqkv, zqkv, zqkv)


NA_QROWS = 8
NA_KROWS = 16


def _na_blocks(rows):
    kh = min(WIN_H, rows)
    assert kh == WIN_H and rows % NA_QROWS == 0 and rows >= NA_KROWS
    blocks, types = [], []
    for m in range(rows // NA_QROWS):
        k0 = int(np.clip(NA_QROWS * m - kh // 2, 0, rows - NA_KROWS))
        tbl = np.full((NA_QROWS, NA_KROWS), 2 * WIN_H - 1, np.int64)
        for ql in range(NA_QROWS):
            rq = NA_QROWS * m + ql
            r0 = int(np.clip(rq - kh // 2, 0, rows - kh))
            for kl in range(NA_KROWS):
                rk = k0 + kl
                if r0 <= rk < r0 + kh:
                    tbl[ql, kl] = rk - rq + (WIN_H - 1)
        key = tbl.tobytes()
        if key not in [t.tobytes() for t in types]:
            types.append(tbl)
        blocks.append((NA_QROWS * m, k0, [t.tobytes() for t in types].index(key)))
    return blocks, types


def _na_kernel(o_in_ref, q_ref, k_ref, v_ref, kc_ref, vc_ref, m_ref, o_ref, bias_ref, *, blocks, types):
    del o_in_ref
    scale = C_HD ** -0.5

    @pl.when(pl.program_id(1) == 0)
    def _():
        for hh in range(LANES // C_HD):
            for t, tbl in enumerate(types):
                for ql in range(NA_QROWS):
                    for kl in range(0, NA_KROWS, 2):
                        pair = jnp.concatenate([m_ref[hh, int(tbl[ql, kl])], m_ref[hh, int(tbl[ql, kl + 1])]], axis=1)
                        bias_ref[hh, t, ql * GRID_W:(ql + 1) * GRID_W, kl * GRID_W:(kl + 2) * GRID_W] = pair

    for hh in range(LANES // C_HD):
        sl = slice(hh * C_HD, (hh + 1) * C_HD)
        kc = kc_ref[0, 0, hh]
        vc = vc_ref[0, 0, hh]
        for q0, k0, t in blocks:
            qs = slice(q0 * GRID_W, (q0 + NA_QROWS) * GRID_W)
            ks = slice(k0 * GRID_W, (k0 + NA_KROWS) * GRID_W)
            qb = q_ref[qs, sl]
            s_loc = _bdot_nt(qb, k_ref[ks, sl]) * scale + bias_ref[hh, t]
            s_ctx = _bdot_nt(qb, kc) * scale
            m = jnp.maximum(jnp.max(s_loc, axis=-1, keepdims=True), jnp.max(s_ctx, axis=-1, keepdims=True))
            e_loc = jnp.exp(s_loc - m)
            e_ctx = jnp.exp(s_ctx - m)
            den = jnp.sum(e_loc, axis=-1, keepdims=True) + jnp.sum(e_ctx, axis=-1, keepdims=True)
            o = _bdot(e_loc, v_ref[ks, sl]) + _bdot(e_ctx, vc)
            o_ref[qs, sl] = o / den


def _na_attn(o_prompt, zqkv, cache_k, cache_v, tiles, blocks, types, layer_j, n_prompt, dec_seq):
    ntok = zqkv.shape[0]
    nb = (ntok - n_prompt) // dec_seq
    boff = n_prompt // dec_seq
    hp = LANES // C_HD
    npair = C_HEADS // hp
    past = cache_k.shape[3]
    return pl.pallas_call(
        functools.partial(_na_kernel, blocks=blocks, types=types),
        grid=(npair, nb),
        in_specs=[pl.BlockSpec(memory_space=pl.ANY),
                  pl.BlockSpec((dec_seq, LANES), lambda p, b: (boff + b, p)),
                  pl.BlockSpec((dec_seq, LANES), lambda p, b: (boff + b, npair + p)),
                  pl.BlockSpec((dec_seq, LANES), lambda p, b: (boff + b, 2 * npair + p)),
                  pl.BlockSpec((1, 1, hp, past, C_HD), lambda p, b: (b, layer_j, p, 0, 0)),
                  pl.BlockSpec((1, 1, hp, past, C_HD), lambda p, b: (b, layer_j, p, 0, 0)),
                  pl.BlockSpec((hp,) + tiles.shape[1:], lambda p, b: (p, 0, 0, 0))],
        out_specs=pl.BlockSpec((dec_seq, LANES), lambda p, b: (boff + b, p)),
        out_shape=jax.ShapeDtypeStruct(o_prompt.shape, F32),
        scratch_shapes=[pltpu.VMEM((hp, len(types), NA_QROWS * GRID_W, NA_KROWS * GRID_W), F32)],
        input_output_aliases={0: 0},
        compiler_params=_cparams(("arbitrary", "arbitrary")),
        name="na_attn",
    )(o_prompt, zqkv, zqkv, zqkv, cache_k, cache_v, tiles)


def _na_bias_kernel(rpb_ref, e_ref, neg_ref, o_ref):
    h, m, l = _split3(rpb_ref[...])
    e = e_ref[...]
    d = lambda y: jnp.dot(y, e, preferred_element_type=F32)
    o_ref[...] = d(h) + (d(m) + d(l)) + neg_ref[...]


def _na_bias_tiles(rpb_l):
    nh, ndr, ndc = rpb_l.shape
    cq = np.arange(GRID_W)
    c0 = np.clip(cq - WIN_W // 2, 0, GRID_W - WIN_W)
    in_win = (cq[None, :] >= c0[:, None]) & (cq[None, :] < c0[:, None] + WIN_W)
    dc = np.clip(cq[None, :] - cq[:, None] + (WIN_W - 1), 0, ndc - 1)
    ndc_pad = 32
    onehot = (dc[None] == np.arange(ndc_pad)[:, None, None]) & in_win[None]
    onehot = jnp.asarray(onehot.reshape(ndc_pad, GRID_W * GRID_W), BF16)
    neg = jnp.asarray(np.where(in_win, 0.0, -np.inf).reshape(1, GRID_W * GRID_W), F32)
    rpb2 = jnp.pad(rpb_l.astype(F32).reshape(nh * ndr, ndc), ((0, 0), (0, ndc_pad - ndc)))
    m = pl.pallas_call(
        _na_bias_kernel,
        out_shape=jax.ShapeDtypeStruct((nh * ndr, GRID_W * GRID_W), F32),
        name="na_bias",
    )(rpb2, onehot, neg).reshape(nh, ndr, GRID_W, GRID_W)
    return jnp.concatenate([m, jnp.full((nh, 1, GRID_W, GRID_W), -jnp.inf, F32)], axis=1)


def _top_desc(x, n):
    vals = []
    for _ in range(n):
        m = jnp.max(x, axis=0, keepdims=True)
        vals.append(m)
        x = jnp.where(x == m, -jnp.inf, x)
    return vals


SUBLANES = 8


def _sort_network(n):
    pairs = []
    p = 1
    while p < n:
        k = p
        while k >= 1:
            for j in range(k % p, n - k, 2 * k):
                for i in range(min(k, n - j - k)):
                    if (i + j) // (2 * p) == (i + j + k) // (2 * p):
                        pairs.append((i + j, i + j + k))
            k //= 2
        p *= 2
    return pairs


def _top_rows(x, n):
    g = [x[SUBLANES * k:SUBLANES * (k + 1)] for k in range(n)]
    for i, j in _sort_network(n):
        g[i], g[j] = jnp.maximum(g[i], g[j]), jnp.minimum(g[i], g[j])
    shift = SUBLANES // 2
    while shift >= 1:
        other = [pltpu.roll(a, shift, axis=0) for a in g]
        g = [jnp.maximum(g[k], other[n - 1 - k]) for k in range(n)]
        stride = n // 2
        while stride >= 1:
            for k in range(n):
                if k & stride == 0:
                    g[k], g[k + stride] = jnp.maximum(g[k], g[k + stride]), jnp.minimum(g[k], g[k + stride])
            stride //= 2
        shift //= 2
    tops = [a[0:1] for a in g]
    nxt = jnp.max(jnp.where(x < tops[n - 1], x, -jnp.inf), axis=0, keepdims=True)
    return tops + [nxt]


def _peer_gate_kernel(x_ref, g_ref, mod_ref, wh_ref, wl_ref, keys_ref, a_ref, b_ref, tau_ref, hn_ref, hq_ref, *,
                      shift_idx, scale_idx):
    kq = PEER_TOPK
    m = mod_ref[0]
    hn = _modnorm(x_ref[...], g_ref[...], m[shift_idx:shift_idx + 1], m[scale_idx:scale_idx + 1])
    hn_hi, hn_lo = _split2(hn)
    hn_ref[0] = hn_hi
    hn_ref[1] = hn_lo
    d = lambda x, y: jnp.dot(x, y, preferred_element_type=F32)

    def query(h):
        cols = slice(h * 2 * N_KEYS, (h + 1) * 2 * N_KEYS)
        wh = wh_ref[:, cols]
        hq_ref[h % 2] = d(hn_ref[0], wh) + (d(hn_ref[0], wl_ref[:, cols]) + d(hn_ref[1], wh))

    query(0)
    for h in range(PEER_HEADS):
        if h + 1 < PEER_HEADS:
            query(h + 1)
        hq = hq_ref[h % 2]
        s = []
        tops = []
        for p in range(2):
            sp = _dot3_nt(keys_ref[h, p], hq[:, p * N_KEYS:(p + 1) * N_KEYS])
            s.append(sp)
            tops.append(_top_rows(sp, kq))
        t0 = jnp.concatenate(tops[0][:kq], axis=0)
        t1 = jnp.concatenate(tops[1][:kq], axis=0)
        ninf = jnp.full((4, t0.shape[1]), -jnp.inf, F32)
        cand = jnp.concatenate([
            tops[0][0] + t1,
            tops[0][1] + t1[0:8],
            tops[0][2] + t1[0:8],
            tops[0][3] + t1[0:8],
            t0[8:16] + tops[1][0],
            t0[4:8] + tops[1][0], t0[4:8] + tops[1][1],
            t0[4:8] + tops[1][2], ninf], axis=0)
        best = _top_desc(cand, kq + 1)
        mx = best[0]
        z = jnp.zeros_like(mx)
        for r in range(kq):
            z = z + jnp.exp(best[r] - mx)
        nxt = jnp.maximum(best[kq], jnp.maximum(tops[0][kq] + tops[1][0], tops[0][0] + tops[1][kq]))
        thr = 0.5 * (best[kq - 1] + nxt)
        inv_z = 1.0 / z
        a_ref[h] = jnp.exp(s[0] - tops[0][0]) * inv_z
        b_ref[h] = jnp.exp(s[1] - tops[1][0]).astype(BF16)
        tau_ref[h:h + 1, :] = jnp.exp(thr - mx) * inv_z


def _peer_gates(x, g, mod, wq, keys, row_fn, tmk, shift_idx, scale_idx):
    ntok, d = x.shape
    nq = wq.shape[1]
    wh = wq.astype(BF16)
    wl = (wq - wh.astype(F32)).astype(BF16)
    return pl.pallas_call(
        functools.partial(_peer_gate_kernel, shift_idx=shift_idx, scale_idx=scale_idx),
        grid=(ntok // tmk,),
        in_specs=[pl.BlockSpec((tmk, d), lambda i: (i, 0)),
                  pl.BlockSpec((1, d), lambda i: (0, 0)),
                  pl.BlockSpec((1, N_MOD, d), lambda i: (row_fn(i), 0, 0)),
                  pl.BlockSpec((d, nq), lambda i: (0, 0)),
                  pl.BlockSpec((d, nq), lambda i: (0, 0)),
                  pl.BlockSpec(keys.shape, lambda i: (0, 0, 0, 0))],
        out_specs=[pl.BlockSpec((PEER_HEADS, N_KEYS, tmk), lambda i: (0, 0, i)),
                   pl.BlockSpec((PEER_HEADS, N_KEYS, tmk), lambda i: (0, 0, i)),
                   pl.BlockSpec((PEER_HEADS, tmk), lambda i: (0, i))],
        out_shape=[jax.ShapeDtypeStruct((PEER_HEADS, N_KEYS, ntok), F32),
                   jax.ShapeDtypeStruct((PEER_HEADS, N_KEYS, ntok), BF16),
                   jax.ShapeDtypeStruct((PEER_HEADS, ntok), F32)],
        scratch_shapes=[pltpu.VMEM((2, tmk, d), BF16),
                        pltpu.VMEM((2, tmk, 2 * N_KEYS), F32)],
        compiler_params=_cparams(("arbitrary",)),
        name="peer_gates",
    )(x, g.reshape(1, d), mod, wh, wl, keys)


BF16_ROWS = 16
PEER_TOK_SUB = 256


def _peer_mix_kernel(x_ref, g_ref, mod_ref, u0_ref, u_ref, vt_ref, a_ref, b_ref, tau_ref, y_ref, h_ref, acc_ref,
                     act_ref, ht_ref, *, shift_idx, scale_idx, gate_idx):
    e = pl.program_id(1)
    te = u_ref.shape[0]
    tm = x_ref.shape[0]
    nsub = te // N_KEYS
    nq = tm // PEER_TOK_SUB
    cols = [slice(q * PEER_TOK_SUB, (q + 1) * PEER_TOK_SUB) for q in range(nq)]

    def activations(u, cs):
        act_ref[:, cs] = jnp.dot(u[...], h_ref[:, cs], preferred_element_type=F32)

    @pl.when(e == 0)
    def _():
        m = mod_ref[0]
        h = _modnorm(x_ref[...], g_ref[...], m[shift_idx:shift_idx + 1], m[scale_idx:scale_idx + 1])
        h_ref[...] = h.T.astype(BF16)
        acc_ref[...] = jnp.zeros_like(acc_ref)
        for cs in cols:
            activations(u0_ref, cs)

    grp = (N_KEYS // BF16_ROWS, BF16_ROWS, PEER_TOK_SUB)
    bcast = lambda r: jnp.broadcast_to(r, (BF16_ROWS, PEER_TOK_SUB)).astype(BF16)[None]
    zero = jnp.zeros(grp, BF16)
    for q, cs in enumerate(cols):
        taus = [bcast(tau_ref[h:h + 1, cs]) for h in range(PEER_HEADS)]
        for ii in range(nsub):
            i = e * nsub + ii
            rows = slice(ii * N_KEYS, (ii + 1) * N_KEYS)
            gsum = None
            for h in range(PEER_HEADS):
                w = bcast(a_ref[h, pl.ds(i, 1), cs]) * b_ref[h, :, cs].reshape(grp)
                term = jnp.where(w >= taus[h], w, zero)
                gsum = term if gsum is None else gsum + term
            act = jax.nn.gelu(act_ref[rows, cs].astype(BF16)).reshape(grp)
            ht_ref[q % 2, rows, :] = (act * gsum).reshape(N_KEYS, PEER_TOK_SUB)
        activations(u_ref, cs)
        acc_ref[:, cs] += jnp.dot(vt_ref[...], ht_ref[q % 2], preferred_element_type=F32)

    @pl.when(e == pl.num_programs(1) - 1)
    def _():
        g = mod_ref[0][gate_idx:gate_idx + 1]
        y_ref[...] = x_ref[...] + g * acc_ref[...].T


def _peer_mix(x, g, mod, u_bf, vt_bf, a, b, tau, row_fn, tm, te, shift_idx, scale_idx, gate_idx):
    ntok, d = x.shape
    nexp = u_bf.shape[0]
    last = nexp // te - 1
    return pl.pallas_call(
        functools.partial(_peer_mix_kernel, shift_idx=shift_idx, scale_idx=scale_idx, gate_idx=gate_idx),
        grid=(ntok // tm, nexp // te),
        in_specs=[pl.BlockSpec((tm, d), lambda t, e: (t, 0)),
                  pl.BlockSpec((1, d), lambda t, e: (0, 0)),
                  pl.BlockSpec((1, N_MOD, d), lambda t, e: (row_fn(t), 0, 0)),
                  pl.BlockSpec((te, d), lambda t, e: (0, 0)),
                  pl.BlockSpec((te, d), lambda t, e: (jnp.minimum(e + 1, last), 0)),
                  pl.BlockSpec((d, te), lambda t, e: (0, e)),
                  pl.BlockSpec((PEER_HEADS, N_KEYS, tm), lambda t, e: (0, 0, t)),
                  pl.BlockSpec((PEER_HEADS, N_KEYS, tm), lambda t, e: (0, 0, t)),
                  pl.BlockSpec((PEER_HEADS, tm), lambda t, e: (0, t))],
        out_specs=pl.BlockSpec((tm, d), lambda t, e: (t, 0)),
        out_shape=jax.ShapeDtypeStruct((ntok, d), F32),
        scratch_shapes=[pltpu.VMEM((d, tm), BF16),
                        pltpu.VMEM((d, tm), F32),
                        pltpu.VMEM((te, tm), F32),
                        pltpu.VMEM((2, te, PEER_TOK_SUB), BF16)],
        compiler_params=_cparams(("arbitrary", "arbitrary")),
        name="peer_mix",
    )(x, g.reshape(1, d), mod, u_bf, u_bf, vt_bf, a, b, tau)


def _seq_tables(n_prompt, seq, n_dec, dec_seq, tb):
    seq_of, first, last, cached, has = [], [], [], [], []
    sid = 0
    for count, length, is_dec in ((n_prompt // seq, seq, 0), (n_dec, dec_seq, 1)):
        nb = length // tb
        for k in range(count):
            for b in range(nb):
                seq_of.append(sid)
                first.append(int(b == 0))
                last.append(int(b == nb - 1))
                cached.append(k if is_dec else 0)
                has.append(is_dec)
            sid += 1
    nblk = len(seq_of)
    fwd = np.array([list(range(nblk)), seq_of, first, cached, has], np.int32)
    order = list(range(nblk - 1, -1, -1))
    pick = lambda a: [a[i] for i in order]
    bwd = np.array([order, pick(seq_of), pick(last), pick(cached), pick(has)], np.int32)
    halo = np.array([[1 - f for f in first], [1 - l for l in last]], np.int32)
    return fwd, bwd, halo, sid


def kernel(x_prompt, x_sample, cache_k, cache_v, state_fwd, state_bwd, c, c_ctx, norm_g, w_mod, b_mod, w_in_ab, hg_lb,
           hg_norm_g, cv_w, cv_b, cv_ln_g, cv_ln_b, w_out_ab, w_qkv, qk_g, rpb, w_out_na, peer_wq, peer_keys, peer_u,
           peer_v):
    bp, seq, d = x_prompt.shape
    nd, dec_seq, _ = x_sample.shape
    depth = w_mod.shape[0]
    n_prompt = bp * seq
    ntok = n_prompt + nd * dec_seq
    assert n_prompt % dec_seq == 0 and seq % TOK_BLK == 0 and dec_seq % TOK_BLK == 0
    rows = dec_seq // GRID_W
    n_qk = A_HEADS * A_DK
    n_v = A_HEADS * A_DV
    n_b = d - n_v

    tile = lambda limit: max(t for t in (128, 256, 512, 1024) if t <= limit and n_prompt % t == 0 and dec_seq % t == 0)
    tm_lin = tile(1024)
    tm_out = tile(512)
    tm_peer = tile(1024)
    te_peer = 512
    tmk = 256

    x = jnp.concatenate([x_prompt.reshape(n_prompt, d), x_sample.reshape(nd * dec_seq, d)], axis=0)
    cond8 = jnp.zeros((8, d), F32).at[0].set(c_ctx).at[1:1 + nd].set(c)
    mods = _modulation(cond8, w_mod, b_mod).reshape(depth, 8, N_MOD, d)

    fwd_tbl, bwd_tbl, halo_tbl, nseq = _seq_tables(n_prompt, seq, nd, dec_seq, TOK_BLK)
    fwd_tbl, bwd_tbl, halo_tbl = jnp.asarray(fwd_tbl), jnp.asarray(bwd_tbl), jnp.asarray(halo_tbl)
    hg_lb_t = jnp.transpose(hg_lb, (1, 0, 2))

    row = lambda tm: _mod_row(n_prompt, dec_seq, tm)
    tn_qkv = 512
    bd = (jnp.arange(tn_qkv)[:, None] // C_HD == jnp.arange(tn_qkv)[None, :] // C_HD).astype(BF16) * (1.0 / C_HD)

    new_k, new_v, new_sf, new_sb = [], [], [], []
    for l in range(depth):
        j = l // 2
        mod = mods[l]
        if l % 2 == 0:
            z = _normlin(x, norm_g[l, 0], mod, w_in_ab[j], row(tm_lin), tm_lin, 512, 0, 1)
            blk = lambda col: col // A_DK
            o_f, sf = _gla(z, hg_lb_t, state_fwd, fwd_tbl, nseq, reverse=False, layer_j=j,
                           qcol=0, fcol=blk(n_qk), vcol=blk(3 * n_qk))
            o_b, sb = _gla(z, hg_lb_t, state_bwd, bwd_tbl, nseq, reverse=True, layer_j=j,
                           qcol=0, fcol=blk(2 * n_qk), vcol=blk(3 * n_qk))
            new_sf.append(sf[:bp])
            new_sb.append(sb[:bp])
            cbase = 3 * n_qk + 2 * n_v
            o_c = _conv(z, halo_tbl, cv_w[j], cv_b[j], cv_ln_g[j], cv_ln_b[j],
                        acol=cbase // n_b, gcol=(cbase + n_b) // n_b)
            x = _even_out(x, mod, o_f, o_b, z, o_c, hg_norm_g[j], w_out_ab[j], row(tm_out), tm_out, 2,
                          (3 * n_qk + n_v) // n_v)
        else:
            wq = C_HEADS * C_HD
            qg = jnp.concatenate([jnp.tile(qk_g[j, 0], C_HEADS), jnp.tile(qk_g[j, 1], C_HEADS)]).reshape(1, 2 * wq)
            z = _normlin(x, norm_g[l, 0], mod, w_qkv[j], row(tm_lin), tm_lin, tn_qkv, 0, 1, qk=(bd, qg, 2 * wq))
            kv = z[:n_prompt].reshape(bp, seq, 3, C_HEADS, C_HD)
            new_k.append(jnp.transpose(kv[:, :, 1], (0, 2, 1, 3)))
            new_v.append(jnp.transpose(kv[:, :, 2], (0, 2, 1, 3)))
            o = _ctx_attn(z, n_prompt, seq)
            na_blocks, na_types = _na_blocks(rows)
            o = _na_attn(o, z, cache_k, cache_v, _na_bias_tiles(rpb[j]), na_blocks, na_types, j, n_prompt, dec_seq)
            x = _lin_out(x, mod, o, w_out_na[j], row(tm_out), tm_out, 2)
        a, b, tau = _peer_gates(x, norm_g[l, 1], mod, peer_wq[l], peer_keys[l], row(tmk), tmk, 3, 4)
        x = _peer_mix(x, norm_g[l, 1], mod, peer_u[l].astype(BF16), peer_v[l].T.astype(BF16), a, b, tau,
                      row(tm_peer), tm_peer, te_peer, 3, 4, 5)

    yp = x[:n_prompt].reshape(bp, seq, d)
    ys = x[n_prompt:].reshape(nd, dec_seq, d)
    return (yp, ys, jnp.stack(new_k, axis=1), jnp.stack(new_v, axis=1),
            jnp.stack(new_sf, axis=1), jnp.stack(new_sb, axis=1))
```

```python
import functools

import numpy as np
import jax
import jax.numpy as jnp
from jax import lax
from jax.experimental import pallas as pl
from jax.experimental.pallas import tpu as pltpu

F32 = jnp.float32
BF16 = jnp.bfloat16

GRID_W = 64
A_HEADS = 4
A_DK = 128
A_DV = 128
CONV_W = 31
GLA_CHUNK = 16
C_HEADS = 16
C_HD = 64
WIN_H = 8
WIN_W = 16
PEER_HEADS = 8
N_KEYS = 128
PEER_TOPK = 16
N_MOD = 6
EPS = 1e-6

LANES = 128
TOK_BLK = 256
CONV_HALO = 16
VMEM_LIMIT = 56 * 1024 * 1024


def _cparams(sem):
    return pltpu.CompilerParams(dimension_semantics=sem, vmem_limit_bytes=VMEM_LIMIT)


def _bdot(a, b):
    return jnp.dot(a.astype(BF16), b.astype(BF16), preferred_element_type=F32)


def _bdot_nt(a, b):
    return lax.dot_general(a.astype(BF16), b.astype(BF16), (((1,), (1,)), ((), ())), preferred_element_type=F32)


def _split2(a):
    hi = a.astype(BF16)
    lo = (a - hi.astype(F32)).astype(BF16)
    return hi, lo


def _split3(a):
    hi = a.astype(BF16)
    r = a - hi.astype(F32)
    mid = r.astype(BF16)
    lo = (r - mid.astype(F32)).astype(BF16)
    return hi, mid, lo


def _dot3(a, b):
    ah, al = _split2(a)
    bh, bl = _split2(b)
    d = lambda x, y: jnp.dot(x, y, preferred_element_type=F32)
    return d(ah, bh) + (d(ah, bl) + d(al, bh))


def _dot3_nt(a, b):
    ah, al = _split2(a)
    bh, bl = _split2(b)
    d = lambda x, y: lax.dot_general(x, y, (((1,), (1,)), ((), ())), preferred_element_type=F32)
    return d(ah, bh) + (d(ah, bl) + d(al, bh))


def _exact_left(mask_bf16, x):
    h, m, l = _split3(x)
    d = lambda y: jnp.dot(mask_bf16, y, preferred_element_type=F32)
    return d(h) + (d(m) + d(l))


def _mod_row(n_prompt, dec_seq, tm):
    def row(i):
        tok = i * tm
        return jnp.where(tok < n_prompt, 0, 1 + (tok - n_prompt) // dec_seq)
    return row


def _modnorm(x, g, shift, scale):
    ms = jnp.mean(x * x, axis=-1, keepdims=True)
    return x * lax.rsqrt(ms + EPS) * g * (1.0 + scale) + shift


def _mod_kernel(c_ref, w_ref, b_ref, o_ref):
    s = jax.nn.silu(c_ref[...])
    o_ref[0] = _dot3(s, w_ref[0]) + b_ref[0]


def _modulation(cond8, w_mod, b_mod):
    depth, d, n = w_mod.shape
    tn = 1536
    return pl.pallas_call(
        _mod_kernel,
        grid=(depth, n // tn),
        in_specs=[pl.BlockSpec((8, d), lambda l, j: (0, 0)),
                  pl.BlockSpec((1, d, tn), lambda l, j: (l, 0, j)),
                  pl.BlockSpec((1, 1, tn), lambda l, j: (l, 0, j))],
        out_specs=pl.BlockSpec((1, 8, tn), lambda l, j: (l, 0, j)),
        out_shape=jax.ShapeDtypeStruct((depth, 8, n), F32),
        compiler_params=_cparams(("arbitrary", "arbitrary")),
        name="modulation",
    )(cond8, w_mod, b_mod.reshape(depth, 1, n))


def _normlin_kernel(x_ref, g_ref, mod_ref, w_ref, *rest, shift_idx, scale_idx, qk_tiles):
    if qk_tiles:
        bd_ref, qg_ref, o_ref, h_ref = rest
    else:
        o_ref, h_ref = rest
    j = pl.program_id(1)

    @pl.when(j == 0)
    def _():
        m = mod_ref[0]
        h = _modnorm(x_ref[...], g_ref[...], m[shift_idx:shift_idx + 1], m[scale_idx:scale_idx + 1])
        h_ref[...] = h.astype(BF16)

    acc = jnp.dot(h_ref[...], w_ref[...].astype(BF16), preferred_element_type=F32)

    if qk_tiles:
        @pl.when(j < qk_tiles)
        def _():
            sh, sl = _split2(acc * acc)
            bd = bd_ref[...]
            ms = jnp.dot(sh, bd, preferred_element_type=F32) + jnp.dot(sl, bd, preferred_element_type=F32)
            o_ref[...] = acc * lax.rsqrt(ms + EPS) * qg_ref[...]

        @pl.when(j >= qk_tiles)
        def _():
            o_ref[...] = acc
    else:
        o_ref[...] = acc


def _normlin(x, g, mod, w, row_fn, tm, tn, shift_idx, scale_idx, qk=None):
    ntok, d = x.shape
    n = w.shape[1]
    in_specs = [pl.BlockSpec((tm, d), lambda i, j: (i, 0)),
                pl.BlockSpec((1, d), lambda i, j: (0, 0)),
                pl.BlockSpec((1, N_MOD, d), lambda i, j: (row_fn(i), 0, 0)),
                pl.BlockSpec((d, tn), lambda i, j: (0, j))]
    args = [x, g.reshape(1, d), mod, w]
    qk_tiles = 0
    if qk is not None:
        bd, qg, n_qk = qk
        qk_tiles = n_qk // tn
        in_specs += [pl.BlockSpec((tn, tn), lambda i, j: (0, 0)),
                     pl.BlockSpec((1, tn), lambda i, j: (0, jnp.minimum(j, qk_tiles - 1)))]
        args += [bd, qg]
    return pl.pallas_call(
        functools.partial(_normlin_kernel, shift_idx=shift_idx, scale_idx=scale_idx, qk_tiles=qk_tiles),
        grid=(ntok // tm, n // tn),
        in_specs=in_specs,
        out_specs=pl.BlockSpec((tm, tn), lambda i, j: (i, j)),
        out_shape=jax.ShapeDtypeStruct((ntok, n), F32),
        scratch_shapes=[pltpu.VMEM((tm, d), BF16)],
        compiler_params=_cparams(("arbitrary", "arbitrary")),
        name="normlin",
    )(*args)


GLA_HEADS_PER_STEP = 2


def _gla_kernel(tbl_ref, zq_ref, zf_ref, zv_ref, lb_ref, s0_ref, o_ref, sfin_ref, st_ref, *, reverse, layer_j):
    step = pl.program_id(1)
    tb = zq_ref.shape[0]
    nchunk = tb // GLA_CHUNK
    nh = zq_ref.shape[1] // A_DK

    @pl.when((tbl_ref[2, step] == 1) & (tbl_ref[4, step] == 1))
    def _():
        for hh in range(nh):
            st_ref[hh] = s0_ref[0, 0, hh].T

    @pl.when((tbl_ref[2, step] == 1) & (tbl_ref[4, step] == 0))
    def _():
        st_ref[...] = jnp.zeros_like(st_ref)

    row = lax.broadcasted_iota(jnp.int32, (tb, tb), 0)
    col = lax.broadcasted_iota(jnp.int32, (tb, tb), 1)
    same = (row // GLA_CHUNK) == (col // GLA_CHUNK)
    if reverse:
        incl = same & (col >= row)
        rest = same & (col < row)
    else:
        incl = same & (col <= row)
        rest = same & (col > row)
    incl_m = jnp.where(incl, 1.0, 0.0).astype(BF16)
    rest_m = jnp.where(rest, 1.0, 0.0).astype(BF16)
    ridx = lax.broadcasted_iota(jnp.int32, (tb, A_DK), 0) // GLA_CHUNK
    cidx_l = lax.broadcasted_iota(jnp.int32, (A_DV, tb), 1) // GLA_CHUNK

    lbx = lb_ref[0]
    e = jnp.exp(lbx - jnp.max(lbx, axis=0, keepdims=True))
    p = e / jnp.sum(e, axis=0, keepdims=True)
    lb_all = jnp.zeros((1, lbx.shape[1]), F32)
    for i in range(1, layer_j + 1):
        lb_all = lb_all + p[i:i + 1]

    for hh in range(nh):
        sl = slice(hh * A_DK, (hh + 1) * A_DK)
        lb = lb_all[:, sl]
        z = zf_ref[:, sl]
        log_sig = jnp.minimum(z, 0.0) - jnp.log1p(jnp.exp(-jnp.abs(z)))
        a_ = jnp.log1p(-lb) + log_sig
        b_ = jnp.log(lb)
        logf = jnp.maximum(a_, b_) + jnp.log1p(jnp.exp(-jnp.abs(a_ - b_)))
        k = (1.0 - lb) * jax.nn.sigmoid(-z)
        q = zq_ref[:, sl] * (A_DK ** -0.5)
        v = zv_ref[:, sl]

        bc = _exact_left(incl_m, logf)
        brem = _exact_left(rest_m, logf)
        qd = q * jnp.exp(bc)
        kinv = k * jnp.exp(-bc)
        att = jnp.where(incl, _bdot_nt(qd, kinv), 0.0)
        o_intra = _bdot(att, v)
        klast = k * jnp.exp(brem)
        vt = v.T
        order = list(range(nchunk - 1, -1, -1) if reverse else range(nchunk))
        upd = []
        for pos in range(0, nchunk, 2):
            w2 = jnp.concatenate([jnp.where(ridx == order[pos], klast, 0.0),
                                  jnp.where(ridx == order[pos + 1], klast, 0.0)], axis=1)
            u2 = _bdot(vt, w2)
            upd += [u2[:, :A_DK], u2[:, A_DK:]]
        st = st_ref[hh]
        starts = []
        for pos, c in enumerate(order):
            starts.append(st)
            last = c * GLA_CHUNK if reverse else (c + 1) * GLA_CHUNK - 1
            a_c = jnp.exp(bc[last:last + 1])
            st = a_c * st + upd[pos]
        qdt = qd.T
        qdt_blocks = jnp.concatenate([jnp.where(cidx_l == c, qdt, 0.0) for c in order], axis=0)
        o_inter = _bdot(jnp.concatenate(starts, axis=1), qdt_blocks).T
        st_ref[hh] = st
        o_ref[:, sl] = o_intra + o_inter
        sfin_ref[0, hh] = st.T


def _gla(z, hg_lb_t, s0, tbl, nseq, *, reverse, layer_j, qcol, fcol, vcol):
    ntok = z.shape[0]
    nsteps = tbl.shape[1]
    tb = TOK_BLK
    n_rec = hg_lb_t.shape[1]
    d = 1 if reverse else 0
    nh = GLA_HEADS_PER_STEP
    w = nh * A_DK
    qb, fb, vb = qcol * A_DK // w, fcol * A_DK // w, vcol * A_DK // w
    grid_spec = pltpu.PrefetchScalarGridSpec(
        num_scalar_prefetch=1,
        grid=(A_HEADS // nh, nsteps),
        in_specs=[pl.BlockSpec((tb, w), lambda h, s, t: (t[0, s], qb + h)),
                  pl.BlockSpec((tb, w), lambda h, s, t: (t[0, s], fb + h)),
                  pl.BlockSpec((tb, w), lambda h, s, t: (t[0, s], vb + h)),
                  pl.BlockSpec((1, n_rec, w), lambda h, s, t: (d, 0, h)),
                  pl.BlockSpec((1, 1, nh, A_DK, A_DV), lambda h, s, t: (t[3, s], layer_j, h, 0, 0))],
        out_specs=[pl.BlockSpec((tb, w), lambda h, s, t: (t[0, s], h)),
                   pl.BlockSpec((1, nh, A_DK, A_DV), lambda h, s, t: (t[1, s], h, 0, 0))],
        scratch_shapes=[pltpu.VMEM((nh, A_DV, A_DK), F32)])
    return pl.pallas_call(
        functools.partial(_gla_kernel, reverse=reverse, layer_j=layer_j),
        grid_spec=grid_spec,
        out_shape=[jax.ShapeDtypeStruct((ntok, A_HEADS * A_DV), F32),
                   jax.ShapeDtypeStruct((nseq, A_HEADS, A_DK, A_DV), F32)],
        compiler_params=_cparams(("arbitrary", "arbitrary")),
        name="gla_bwd" if reverse else "gla_fwd",
    )(tbl, z, z, z, hg_lb_t, s0)


def _conv_kernel(tbl_ref, ua_ref, ug_ref, pa_ref, pg_ref, na_ref, ng_ref, w_ref, b_ref, lg_ref, lbias_ref, o_ref,
                 buf_ref):
    i = pl.program_id(0)
    tb = ua_ref.shape[0]
    hl = CONV_HALO
    prev_ok = tbl_ref[0, i] == 1
    next_ok = tbl_ref[1, i] == 1
    buf_ref[hl:hl + tb, :] = ua_ref[...] * jax.nn.sigmoid(ug_ref[...])
    buf_ref[0:hl, :] = jnp.where(prev_ok, pa_ref[...] * jax.nn.sigmoid(pg_ref[...]), 0.0)
    buf_ref[hl + tb:hl + tb + hl, :] = jnp.where(next_ok, na_ref[...] * jax.nn.sigmoid(ng_ref[...]), 0.0)
    w = w_ref[...]
    acc = jnp.zeros(o_ref.shape, F32) + b_ref[...]
    base = hl - CONV_W // 2
    for t in range(CONV_W):
        acc = acc + buf_ref[base + t:base + t + tb, :] * w[t:t + 1]
    mu = jnp.mean(acc, axis=-1, keepdims=True)
    xc = acc - mu
    var = jnp.mean(xc * xc, axis=-1, keepdims=True)
    y = xc * lax.rsqrt(var + EPS) * lg_ref[...] + lbias_ref[...]
    o_ref[...] = jax.nn.silu(y)


def _conv(z, tbl, cv_w, cv_b, ln_g, ln_b, *, acol, gcol):
    ntok = z.shape[0]
    ch = cv_w.shape[1]
    tb = TOK_BLK
    hl = CONV_HALO
    r = tb // hl
    last = ntok // hl - 1
    prev = lambda i, t: jnp.maximum(i * r - 1, 0)
    nxt = lambda i, t: jnp.minimum((i + 1) * r, last)
    grid_spec = pltpu.PrefetchScalarGridSpec(
        num_scalar_prefetch=1,
        grid=(ntok // tb,),
        in_specs=[pl.BlockSpec((tb, ch), lambda i, t: (i, acol)),
                  pl.BlockSpec((tb, ch), lambda i, t: (i, gcol)),
                  pl.BlockSpec((hl, ch), lambda i, t: (prev(i, t), acol)),
                  pl.BlockSpec((hl, ch), lambda i, t: (prev(i, t), gcol)),
                  pl.BlockSpec((hl, ch), lambda i, t: (nxt(i, t), acol)),
                  pl.BlockSpec((hl, ch), lambda i, t: (nxt(i, t), gcol)),
                  pl.BlockSpec((CONV_W, ch), lambda i, t: (0, 0)),
                  pl.BlockSpec((1, ch), lambda i, t: (0, 0)),
                  pl.BlockSpec((1, ch), lambda i, t: (0, 0)),
                  pl.BlockSpec((1, ch), lambda i, t: (0, 0))],
        out_specs=pl.BlockSpec((tb, ch), lambda i, t: (i, 0)),
        scratch_shapes=[pltpu.VMEM((tb + 2 * hl, ch), F32)])
    return pl.pallas_call(
        _conv_kernel,
        grid_spec=grid_spec,
        out_shape=jax.ShapeDtypeStruct((ntok, ch), F32),
        compiler_params=_cparams(("arbitrary",)),
        name="conv_module",
    )(tbl, z, z, z, z, z, z, cv_w, cv_b.reshape(1, ch), ln_g.reshape(1, ch), ln_b.reshape(1, ch))


def _even_out_kernel(x_ref, mod_ref, of_ref, ob_ref, zg_ref, oc_ref, gn_ref, w_ref, y_ref, *, gate_idx):
    o = of_ref[...] + ob_ref[...]
    gate = jax.nn.silu(zg_ref[...])
    gn = gn_ref[...]
    parts = []
    for h in range(A_HEADS):
        oh = o[:, h * A_DV:(h + 1) * A_DV]
        ms = jnp.mean(oh * oh, axis=-1, keepdims=True)
        parts.append(oh * lax.rsqrt(ms + EPS) * gn * gate[:, h * A_DV:(h + 1) * A_DV])
    o_a = jnp.concatenate(parts, axis=-1)
    na = o_a.shape[1]
    acc = _bdot(o_a, w_ref[0:na, :]) + _bdot(oc_ref[...], w_ref[na:, :])
    g = mod_ref[0][gate_idx:gate_idx + 1]
    y_ref[...] = x_ref[...] + g * acc


def _even_out(x, mod, o_f, o_b, z, o_c, gn_g, w_out, row_fn, tm, gate_idx, gcol):
    ntok, d = x.shape
    na = o_f.shape[1]
    nc = o_c.shape[1]
    return pl.pallas_call(
        functools.partial(_even_out_kernel, gate_idx=gate_idx),
        grid=(ntok // tm,),
        in_specs=[pl.BlockSpec((tm, d), lambda i: (i, 0)),
                  pl.BlockSpec((1, N_MOD, d), lambda i: (row_fn(i), 0, 0)),
                  pl.BlockSpec((tm, na), lambda i: (i, 0)),
                  pl.BlockSpec((tm, na), lambda i: (i, 0)),
                  pl.BlockSpec((tm, na), lambda i: (i, gcol)),
                  pl.BlockSpec((tm, nc), lambda i: (i, 0)),
                  pl.BlockSpec((1, A_DV), lambda i: (0, 0)),
                  pl.BlockSpec((na + nc, d), lambda i: (0, 0))],
        out_specs=pl.BlockSpec((tm, d), lambda i: (i, 0)),
        out_shape=jax.ShapeDtypeStruct((ntok, d), F32),
        compiler_params=_cparams(("arbitrary",)),
        name="even_out",
    )(x, mod, o_f, o_b, z, o_c, gn_g.reshape(1, A_DV), w_out)


def _lin_out_kernel(x_ref, mod_ref, o_ref, w_ref, y_ref, *, gate_idx):
    g = mod_ref[0][gate_idx:gate_idx + 1]
    y_ref[...] = x_ref[...] + g * _bdot(o_ref[...], w_ref[...])


def _lin_out(x, mod, o, w, row_fn, tm, gate_idx):
    ntok, d = x.shape
    kdim = o.shape[1]
    return pl.pallas_call(
        functools.partial(_lin_out_kernel, gate_idx=gate_idx),
        grid=(ntok // tm,),
        in_specs=[pl.BlockSpec((tm, d), lambda i: (i, 0)),
                  pl.BlockSpec((1, N_MOD, d), lambda i: (row_fn(i), 0, 0)),
                  pl.BlockSpec((tm, kdim), lambda i: (i, 0)),
                  pl.BlockSpec((kdim, d), lambda i: (0, 0))],
        out_specs=pl.BlockSpec((tm, d), lambda i: (i, 0)),
        out_shape=jax.ShapeDtypeStruct((ntok, d), F32),
        compiler_params=_cparams(("arbitrary",)),
        name="lin_out",
    )(x, mod, o, w)


def _ctx_attn_kernel(q_ref, k_ref, v_ref, o_ref):
    scale = C_HD ** -0.5
    for h in range(C_HEADS):
        sl = slice(h * C_HD, (h + 1) * C_HD)
        s = _bdot_nt(q_ref[:, sl], k_ref[:, sl]) * scale
        e = jnp.exp(s - jnp.max(s, axis=-1, keepdims=True))
        p = e / jnp.sum(e, axis=-1, keepdims=True)
        o_ref[:, sl] = _bdot(p, v_ref[:, sl])


def _ctx_attn(zqkv, n_prompt, seq):
    ntok = zqkv.shape[0]
    w = C_HEADS * C_HD
    return pl.pallas_call(
        _ctx_attn_kernel,
        grid=(n_prompt // seq,),
        in_specs=[pl.BlockSpec((seq, w), lambda b: (b, 0)),
                  pl.BlockSpec((seq, w), lambda b: (b, 1)),
                  pl.BlockSpec((seq, w), lambda b: (b, 2))],
        out_specs=pl.BlockSpec((seq, w), lambda b: (b, 0)),
        out_shape=jax.ShapeDtypeStruct((ntok, w), F32),
        compiler_params=_cparams(("arbitrary",)),
        name="ctx_attn",
    )(zqkv, zqkv, zqkv)


NA_QROWS = 8
NA_KROWS = 16


def _na_blocks(rows):
    kh = min(WIN_H, rows)
    assert kh == WIN_H and rows % NA_QROWS == 0 and rows >= NA_KROWS
    blocks, types = [], []
    for m in range(rows // NA_QROWS):
        k0 = int(np.clip(NA_QROWS * m - kh // 2, 0, rows - NA_KROWS))
        tbl = np.full((NA_QROWS, NA_KROWS), 2 * WIN_H - 1, np.int64)
        for ql in range(NA_QROWS):
            rq = NA_QROWS * m + ql
            r0 = int(np.clip(rq - kh // 2, 0, rows - kh))
            for kl in range(NA_KROWS):
                rk = k0 + kl
                if r0 <= rk < r0 + kh:
                    tbl[ql, kl] = rk - rq + (WIN_H - 1)
        key = tbl.tobytes()
        if key not in [t.tobytes() for t in types]:
            types.append(tbl)
        blocks.append((NA_QROWS * m, k0, [t.tobytes() for t in types].index(key)))
    return blocks, types


def _na_kernel(o_in_ref, q_ref, k_ref, v_ref, kc_ref, vc_ref, m_ref, o_ref, bias_ref, *, blocks, types):
    del o_in_ref
    scale = C_HD ** -0.5

    @pl.when(pl.program_id(1) == 0)
    def _():
        for hh in range(LANES // C_HD):
            for t, tbl in enumerate(types):
                for ql in range(NA_QROWS):
                    for kl in range(0, NA_KROWS, 2):
                        pair = jnp.concatenate([m_ref[hh, int(tbl[ql, kl])], m_ref[hh, int(tbl[ql, kl + 1])]], axis=1)
                        bias_ref[hh, t, ql * GRID_W:(ql + 1) * GRID_W, kl * GRID_W:(kl + 2) * GRID_W] = pair

    for hh in range(LANES // C_HD):
        sl = slice(hh * C_HD, (hh + 1) * C_HD)
        kc = kc_ref[0, 0, hh]
        vc = vc_ref[0, 0, hh]
        for q0, k0, t in blocks:
            qs = slice(q0 * GRID_W, (q0 + NA_QROWS) * GRID_W)
            ks = slice(k0 * GRID_W, (k0 + NA_KROWS) * GRID_W)
            qb = q_ref[qs, sl]
            s_loc = _bdot_nt(qb, k_ref[ks, sl]) * scale + bias_ref[hh, t]
            s_ctx = _bdot_nt(qb, kc) * scale
            m = jnp.maximum(jnp.max(s_loc, axis=-1, keepdims=True), jnp.max(s_ctx, axis=-1, keepdims=True))
            e_loc = jnp.exp(s_loc - m)
            e_ctx = jnp.exp(s_ctx - m)
            den = jnp.sum(e_loc, axis=-1, keepdims=True) + jnp.sum(e_ctx, axis=-1, keepdims=True)
            o = _bdot(e_loc, v_ref[ks, sl]) + _bdot(e_ctx, vc)
            o_ref[qs, sl] = o / den


def _na_attn(o_prompt, zqkv, cache_k, cache_v, tiles, blocks, types, layer_j, n_prompt, dec_seq):
    ntok = zqkv.shape[0]
    nb = (ntok - n_prompt) // dec_seq
    boff = n_prompt // dec_seq
    hp = LANES // C_HD
    npair = C_HEADS // hp
    past = cache_k.shape[3]
    return pl.pallas_call(
        functools.partial(_na_kernel, blocks=blocks, types=types),
        grid=(npair, nb),
        in_specs=[pl.BlockSpec(memory_space=pl.ANY),
                  pl.BlockSpec((dec_seq, LANES), lambda p, b: (boff + b, p)),
                  pl.BlockSpec((dec_seq, LANES), lambda p, b: (boff + b, npair + p)),
                  pl.BlockSpec((dec_seq, LANES), lambda p, b: (boff + b, 2 * npair + p)),
                  pl.BlockSpec((1, 1, hp, past, C_HD), lambda p, b: (b, layer_j, p, 0, 0)),
                  pl.BlockSpec((1, 1, hp, past, C_HD), lambda p, b: (b, layer_j, p, 0, 0)),
                  pl.BlockSpec((hp,) + tiles.shape[1:], lambda p, b: (p, 0, 0, 0))],
        out_specs=pl.BlockSpec((dec_seq, LANES), lambda p, b: (boff + b, p)),
        out_shape=jax.ShapeDtypeStruct(o_prompt.shape, F32),
        scratch_shapes=[pltpu.VMEM((hp, len(types), NA_QROWS * GRID_W, NA_KROWS * GRID_W), F32)],
        input_output_aliases={0: 0},
        compiler_params=_cparams(("arbitrary", "arbitrary")),
        name="na_attn",
    )(o_prompt, zqkv, zqkv, zqkv, cache_k, cache_v, tiles)


def _na_bias_kernel(rpb_ref, e_ref, neg_ref, o_ref):
    h, m, l = _split3(rpb_ref[...])
    e = e_ref[...]
    d = lambda y: jnp.dot(y, e, preferred_element_type=F32)
    o_ref[...] = d(h) + (d(m) + d(l)) + neg_ref[...]


def _na_bias_tiles(rpb_l):
    nh, ndr, ndc = rpb_l.shape
    cq = np.arange(GRID_W)
    c0 = np.clip(cq - WIN_W // 2, 0, GRID_W - WIN_W)
    in_win = (cq[None, :] >= c0[:, None]) & (cq[None, :] < c0[:, None] + WIN_W)
    dc = np.clip(cq[None, :] - cq[:, None] + (WIN_W - 1), 0, ndc - 1)
    ndc_pad = 32
    onehot = (dc[None] == np.arange(ndc_pad)[:, None, None]) & in_win[None]
    onehot = jnp.asarray(onehot.reshape(ndc_pad, GRID_W * GRID_W), BF16)
    neg = jnp.asarray(np.where(in_win, 0.0, -np.inf).reshape(1, GRID_W * GRID_W), F32)
    rpb2 = jnp.pad(rpb_l.astype(F32).reshape(nh * ndr, ndc), ((0, 0), (0, ndc_pad - ndc)))
    m = pl.pallas_call(
        _na_bias_kernel,
        out_shape=jax.ShapeDtypeStruct((nh * ndr, GRID_W * GRID_W), F32),
        name="na_bias",
    )(rpb2, onehot, neg).reshape(nh, ndr, GRID_W, GRID_W)
    return jnp.concatenate([m, jnp.full((nh, 1, GRID_W, GRID_W), -jnp.inf, F32)], axis=1)


def _top_desc(x, n):
    vals = []
    for _ in range(n):
        m = jnp.max(x, axis=0, keepdims=True)
        vals.append(m)
        x = jnp.where(x == m, -jnp.inf, x)
    return vals


SUBLANES = 8


def _sort_network(n):
    pairs = []
    p = 1
    while p < n:
        k = p
        while k >= 1:
            for j in range(k % p, n - k, 2 * k):
                for i in range(min(k, n - j - k)):
                    if (i + j) // (2 * p) == (i + j + k) // (2 * p):
                        pairs.append((i + j, i + j + k))
            k //= 2
        p *= 2
    return pairs


def _top_rows(x, n):
    g = [x[SUBLANES * k:SUBLANES * (k + 1)] for k in range(n)]
    for i, j in _sort_network(n):
        g[i], g[j] = jnp.maximum(g[i], g[j]), jnp.minimum(g[i], g[j])
    shift = SUBLANES // 2
    while shift >= 1:
        other = [pltpu.roll(a, shift, axis=0) for a in g]
        g = [jnp.maximum(g[k], other[n - 1 - k]) for k in range(n)]
        stride = n // 2
        while stride >= 1:
            for k in range(n):
                if k & stride == 0:
                    g[k], g[k + stride] = jnp.maximum(g[k], g[k + stride]), jnp.minimum(g[k], g[k + stride])
            stride //= 2
        shift //= 2
    tops = [a[0:1] for a in g]
    nxt = jnp.max(jnp.where(x < tops[n - 1], x, -jnp.inf), axis=0, keepdims=True)
    return tops + [nxt]


def _peer_gate_kernel(x_ref, g_ref, mod_ref, w_ref, keys_ref, a_ref, b_ref, tau_ref, hn_ref, hq_ref, *,
                      shift_idx, scale_idx):
    kq = PEER_TOPK
    m = mod_ref[0]
    hn = _modnorm(x_ref[...], g_ref[...], m[shift_idx:shift_idx + 1], m[scale_idx:scale_idx + 1])
    hn_ref[...] = hn.astype(BF16)

    def query(h):
        cols = slice(h * 2 * N_KEYS, (h + 1) * 2 * N_KEYS)
        hq_ref[h % 2] = jnp.dot(hn_ref[...], w_ref[:, cols], preferred_element_type=F32)

    query(0)
    for h in range(PEER_HEADS):
        if h + 1 < PEER_HEADS:
            query(h + 1)
        hq = hq_ref[h % 2]
        s = []
        tops = []
        for p in range(2):
            sp = _dot3_nt(keys_ref[h, p], hq[:, p * N_KEYS:(p + 1) * N_KEYS])
            s.append(sp)
            tops.append(_top_rows(sp, kq))
        t0 = jnp.concatenate(tops[0][:kq], axis=0)
        t1 = jnp.concatenate(tops[1][:kq], axis=0)
        ninf = jnp.full((4, t0.shape[1]), -jnp.inf, F32)
        cand = jnp.concatenate([
            tops[0][0] + t1,
            tops[0][1] + t1[0:8],
            tops[0][2] + t1[0:8],
            tops[0][3] + t1[0:8],
            t0[8:16] + tops[1][0],
            t0[4:8] + tops[1][0], t0[4:8] + tops[1][1],
            t0[4:8] + tops[1][2], ninf], axis=0)
        best = _top_desc(cand, kq + 1)
        mx = best[0]
        z = jnp.zeros_like(mx)
        for r in range(kq):
            z = z + jnp.exp(best[r] - mx)
        nxt = jnp.maximum(best[kq], jnp.maximum(tops[0][kq] + tops[1][0], tops[0][0] + tops[1][kq]))
        thr = 0.5 * (best[kq - 1] + nxt)
        inv_z = 1.0 / z
        a_ref[h] = jnp.exp(s[0] - tops[0][0]) * inv_z
        b_ref[h] = jnp.exp(s[1] - tops[1][0]).astype(BF16)
        tau_ref[h:h + 1, :] = jnp.exp(thr - mx) * inv_z


def _peer_gates(x, g, mod, wq, keys, row_fn, tmk, shift_idx, scale_idx):
    ntok, d = x.shape
    nq = wq.shape[1]
    return pl.pallas_call(
        functools.partial(_peer_gate_kernel, shift_idx=shift_idx, scale_idx=scale_idx),
        grid=(ntok // tmk,),
        in_specs=[pl.BlockSpec((tmk, d), lambda i: (i, 0)),
                  pl.BlockSpec((1, d), lambda i: (0, 0)),
                  pl.BlockSpec((1, N_MOD, d), lambda i: (row_fn(i), 0, 0)),
                  pl.BlockSpec((d, nq), lambda i: (0, 0)),
                  pl.BlockSpec(keys.shape, lambda i: (0, 0, 0, 0))],
        out_specs=[pl.BlockSpec((PEER_HEADS, N_KEYS, tmk), lambda i: (0, 0, i)),
                   pl.BlockSpec((PEER_HEADS, N_KEYS, tmk), lambda i: (0, 0, i)),
                   pl.BlockSpec((PEER_HEADS, tmk), lambda i: (0, i))],
        out_shape=[jax.ShapeDtypeStruct((PEER_HEADS, N_KEYS, ntok), F32),
                   jax.ShapeDtypeStruct((PEER_HEADS, N_KEYS, ntok), BF16),
                   jax.ShapeDtypeStruct((PEER_HEADS, ntok), F32)],
        scratch_shapes=[pltpu.VMEM((tmk, d), BF16),
                        pltpu.VMEM((2, tmk, 2 * N_KEYS), F32)],
        compiler_params=_cparams(("arbitrary",)),
        name="peer_gates",
    )(x, g.reshape(1, d), mod, wq.astype(BF16), keys)


BF16_ROWS = 16
PEER_TOK_SUB = 256


def _peer_mix_kernel(x_ref, g_ref, mod_ref, u0_ref, u_ref, vt_ref, a_ref, b_ref, tau_ref, y_ref, h_ref, acc_ref,
                     act_ref, ht_ref, *, shift_idx, scale_idx, gate_idx):
    e = pl.program_id(1)
    te = u_ref.shape[0]
    tm = x_ref.shape[0]
    nsub = te // N_KEYS
    nq = tm // PEER_TOK_SUB
    cols = [slice(q * PEER_TOK_SUB, (q + 1) * PEER_TOK_SUB) for q in range(nq)]

    def activations(u, cs):
        act_ref[:, cs] = jnp.dot(u[...], h_ref[:, cs], preferred_element_type=F32)

    @pl.when(e == 0)
    def _():
        m = mod_ref[0]
        h = _modnorm(x_ref[...], g_ref[...], m[shift_idx:shift_idx + 1], m[scale_idx:scale_idx + 1])
        h_ref[...] = h.T.astype(BF16)
        acc_ref[...] = jnp.zeros_like(acc_ref)
        for cs in cols:
            activations(u0_ref, cs)

    grp = (N_KEYS // BF16_ROWS, BF16_ROWS, PEER_TOK_SUB)
    bcast = lambda r: jnp.broadcast_to(r, (BF16_ROWS, PEER_TOK_SUB)).astype(BF16)[None]
    zero = jnp.zeros(grp, BF16)
    for q, cs in enumerate(cols):
        taus = [bcast(tau_ref[h:h + 1, cs]) for h in range(PEER_HEADS)]
        for ii in range(nsub):
            i = e * nsub + ii
            rows = slice(ii * N_KEYS, (ii + 1) * N_KEYS)
            gsum = None
            for h in range(PEER_HEADS):
                w = bcast(a_ref[h, pl.ds(i, 1), cs]) * b_ref[h, :, cs].reshape(grp)
                term = jnp.where(w >= taus[h], w, zero)
                gsum = term if gsum is None else gsum + term
            act = jax.nn.gelu(act_ref[rows, cs].astype(BF16)).reshape(grp)
            ht_ref[q % 2, rows, :] = (act * gsum).reshape(N_KEYS, PEER_TOK_SUB)
        activations(u_ref, cs)
        acc_ref[:, cs] += jnp.dot(vt_ref[...], ht_ref[q % 2], preferred_element_type=F32)

    @pl.when(e == pl.num_programs(1) - 1)
    def _():
        g = mod_ref[0][gate_idx:gate_idx + 1]
        y_ref[...] = x_ref[...] + g * acc_ref[...].T


def _peer_mix(x, g, mod, u_bf, vt_bf, a, b, tau, row_fn, tm, te, shift_idx, scale_idx, gate_idx):
    ntok, d = x.shape
    nexp = u_bf.shape[0]
    last = nexp // te - 1
    return pl.pallas_call(
        functools.partial(_peer_mix_kernel, shift_idx=shift_idx, scale_idx=scale_idx, gate_idx=gate_idx),
        grid=(ntok // tm, nexp // te),
        in_specs=[pl.BlockSpec((tm, d), lambda t, e: (t, 0)),
                  pl.BlockSpec((1, d), lambda t, e: (0, 0)),
                  pl.BlockSpec((1, N_MOD, d), lambda t, e: (row_fn(t), 0, 0)),
                  pl.BlockSpec((te, d), lambda t, e: (0, 0)),
                  pl.BlockSpec((te, d), lambda t, e: (jnp.minimum(e + 1, last), 0)),
                  pl.BlockSpec((d, te), lambda t, e: (0, e)),
                  pl.BlockSpec((PEER_HEADS, N_KEYS, tm), lambda t, e: (0, 0, t)),
                  pl.BlockSpec((PEER_HEADS, N_KEYS, tm), lambda t, e: (0, 0, t)),
                  pl.BlockSpec((PEER_HEADS, tm), lambda t, e: (0, t))],
        out_specs=pl.BlockSpec((tm, d), lambda t, e: (t, 0)),
        out_shape=jax.ShapeDtypeStruct((ntok, d), F32),
        scratch_shapes=[pltpu.VMEM((d, tm), BF16),
                        pltpu.VMEM((d, tm), F32),
                        pltpu.VMEM((te, tm), F32),
                        pltpu.VMEM((2, te, PEER_TOK_SUB), BF16)],
        compiler_params=_cparams(("arbitrary", "arbitrary")),
        name="peer_mix",
    )(x, g.reshape(1, d), mod, u_bf, u_bf, vt_bf, a, b, tau)


def _seq_tables(n_prompt, seq, n_dec, dec_seq, tb):
    seq_of, first, last, cached, has = [], [], [], [], []
    sid = 0
    for count, length, is_dec in ((n_prompt // seq, seq, 0), (n_dec, dec_seq, 1)):
        nb = length // tb
        for k in range(count):
            for b in range(nb):
                seq_of.append(sid)
                first.append(int(b == 0))
                last.append(int(b == nb - 1))
                cached.append(k if is_dec else 0)
                has.append(is_dec)
            sid += 1
    nblk = len(seq_of)
    fwd = np.array([list(range(nblk)), seq_of, first, cached, has], np.int32)
    order = list(range(nblk - 1, -1, -1))
    pick = lambda a: [a[i] for i in order]
    bwd = np.array([order, pick(seq_of), pick(last), pick(cached), pick(has)], np.int32)
    halo = np.array([[1 - f for f in first], [1 - l for l in last]], np.int32)
    return fwd, bwd, halo, sid


def kernel(x_prompt, x_sample, cache_k, cache_v, state_fwd, state_bwd, c, c_ctx, norm_g, w_mod, b_mod, w_in_ab, hg_lb,
           hg_norm_g, cv_w, cv_b, cv_ln_g, cv_ln_b, w_out_ab, w_qkv, qk_g, rpb, w_out_na, peer_wq, peer_keys, peer_u,
           peer_v):
    bp, seq, d = x_prompt.shape
    nd, dec_seq, _ = x_sample.shape
    depth = w_mod.shape[0]
    n_prompt = bp * seq
    ntok = n_prompt + nd * dec_seq
    assert n_prompt % dec_seq == 0 and seq % TOK_BLK == 0 and dec_seq % TOK_BLK == 0
    rows = dec_seq // GRID_W
    n_qk = A_HEADS * A_DK
    n_v = A_HEADS * A_DV
    n_b = d - n_v

    tile = lambda limit: max(t for t in (128, 256, 512, 1024) if t <= limit and n_prompt % t == 0 and dec_seq % t == 0)
    tm_lin = tile(1024)
    tm_out = tile(512)
    tm_peer = tile(1024)
    te_peer = 512
    tmk = 256

    x = jnp.concatenate([x_prompt.reshape(n_prompt, d), x_sample.reshape(nd * dec_seq, d)], axis=0)
    cond8 = jnp.zeros((8, d), F32).at[0].set(c_ctx).at[1:1 + nd].set(c)
    mods = _modulation(cond8, w_mod, b_mod).reshape(depth, 8, N_MOD, d)

    fwd_tbl, bwd_tbl, halo_tbl, nseq = _seq_tables(n_prompt, seq, nd, dec_seq, TOK_BLK)
    fwd_tbl, bwd_tbl, halo_tbl = jnp.asarray(fwd_tbl), jnp.asarray(bwd_tbl), jnp.asarray(halo_tbl)
    hg_lb_t = jnp.transpose(hg_lb, (1, 0, 2))

    row = lambda tm: _mod_row(n_prompt, dec_seq, tm)
    tn_qkv = 512
    bd = (jnp.arange(tn_qkv)[:, None] // C_HD == jnp.arange(tn_qkv)[None, :] // C_HD).astype(BF16) * (1.0 / C_HD)

    new_k, new_v, new_sf, new_sb = [], [], [], []
    for l in range(depth):
        j = l // 2
        mod = mods[l]
        if l % 2 == 0:
            z = _normlin(x, norm_g[l, 0], mod, w_in_ab[j], row(tm_lin), tm_lin, 512, 0, 1)
            blk = lambda col: col // A_DK
            o_f, sf = _gla(z, hg_lb_t, state_fwd, fwd_tbl, nseq, reverse=False, layer_j=j,
                           qcol=0, fcol=blk(n_qk), vcol=blk(3 * n_qk))
            o_b, sb = _gla(z, hg_lb_t, state_bwd, bwd_tbl, nseq, reverse=True, layer_j=j,
                           qcol=0, fcol=blk(2 * n_qk), vcol=blk(3 * n_qk))
            new_sf.append(sf[:bp])
            new_sb.append(sb[:bp])
            cbase = 3 * n_qk + 2 * n_v
            o_c = _conv(z, halo_tbl, cv_w[j], cv_b[j], cv_ln_g[j], cv_ln_b[j],
                        acol=cbase // n_b, gcol=(cbase + n_b) // n_b)
            x = _even_out(x, mod, o_f, o_b, z, o_c, hg_norm_g[j], w_out_ab[j], row(tm_out), tm_out, 2,
                          (3 * n_qk + n_v) // n_v)
        else:
            wq = C_HEADS * C_HD
            qg = jnp.concatenate([jnp.tile(qk_g[j, 0], C_HEADS), jnp.tile(qk_g[j, 1], C_HEADS)]).reshape(1, 2 * wq)
            z = _normlin(x, norm_g[l, 0], mod, w_qkv[j], row(tm_lin), tm_lin, tn_qkv, 0, 1, qk=(bd, qg, 2 * wq))
            kv = z[:n_prompt].reshape(bp, seq, 3, C_HEADS, C_HD)
            new_k.append(jnp.transpose(kv[:, :, 1], (0, 2, 1, 3)))
            new_v.append(jnp.transpose(kv[:, :, 2], (0, 2, 1, 3)))
            o = _ctx_attn(z, n_prompt, seq)
            na_blocks, na_types = _na_blocks(rows)
            o = _na_attn(o, z, cache_k, cache_v, _na_bias_tiles(rpb[j]), na_blocks, na_types, j, n_prompt, dec_seq)
            x = _lin_out(x, mod, o, w_out_na[j], row(tm_out), tm_out, 2)
        a, b, tau = _peer_gates(x, norm_g[l, 1], mod, peer_wq[l], peer_keys[l], row(tmk), tmk, 3, 4)
        x = _peer_mix(x, norm_g[l, 1], mod, peer_u[l].astype(BF16), peer_v[l].T.astype(BF16), a, b, tau,
                      row(tm_peer), tm_peer, te_peer, 3, 4, 5)

    yp = x[:n_prompt].reshape(bp, seq, d)
    ys = x[n_prompt:].reshape(nd, dec_seq, d)
    return (yp, ys, jnp.stack(new_k, axis=1), jnp.stack(new_v, axis=1),
            jnp.stack(new_sf, axis=1), jnp.stack(new_sb, axis=1))
```

```python
import functools

import numpy as np
import jax
import jax.numpy as jnp
from jax import lax
from jax.experimental import pallas as pl
from jax.experimental.pallas import tpu as pltpu

F32 = jnp.float32
BF16 = jnp.bfloat16

GRID_W = 64
A_HEADS = 4
A_DK = 128
A_DV = 128
CONV_W = 31
GLA_CHUNK = 16
C_HEADS = 16
C_HD = 64
WIN_H = 8
WIN_W = 16
PEER_HEADS = 8
N_KEYS = 128
PEER_TOPK = 16
N_MOD = 6
EPS = 1e-6

LANES = 128
TOK_BLK = 256
CONV_HALO = 16
VMEM_LIMIT = 56 * 1024 * 1024


def _cparams(sem):
    return pltpu.CompilerParams(dimension_semantics=sem, vmem_limit_bytes=VMEM_LIMIT)


def _bdot(a, b):
    return jnp.dot(a.astype(BF16), b.astype(BF16), preferred_element_type=F32)


def _bdot_nt(a, b):
    return lax.dot_general(a.astype(BF16), b.astype(BF16), (((1,), (1,)), ((), ())), preferred_element_type=F32)


def _split2(a):
    hi = a.astype(BF16)
    lo = (a - hi.astype(F32)).astype(BF16)
    return hi, lo


def _split3(a):
    hi = a.astype(BF16)
    r = a - hi.astype(F32)
    mid = r.astype(BF16)
    lo = (r - mid.astype(F32)).astype(BF16)
    return hi, mid, lo


def _dot3(a, b):
    ah, al = _split2(a)
    bh, bl = _split2(b)
    d = lambda x, y: jnp.dot(x, y, preferred_element_type=F32)
    return d(ah, bh) + (d(ah, bl) + d(al, bh))


def _dot3_nt(a, b):
    ah, al = _split2(a)
    bh, bl = _split2(b)
    d = lambda x, y: lax.dot_general(x, y, (((1,), (1,)), ((), ())), preferred_element_type=F32)
    return d(ah, bh) + (d(ah, bl) + d(al, bh))


def _exact_left(mask_bf16, x):
    h, m, l = _split3(x)
    d = lambda y: jnp.dot(mask_bf16, y, preferred_element_type=F32)
    return d(h) + (d(m) + d(l))


def _mod_row(n_prompt, dec_seq, tm):
    def row(i):
        tok = i * tm
        return jnp.where(tok < n_prompt, 0, 1 + (tok - n_prompt) // dec_seq)
    return row


def _modnorm(x, g, shift, scale):
    ms = jnp.mean(x * x, axis=-1, keepdims=True)
    return x * lax.rsqrt(ms + EPS) * g * (1.0 + scale) + shift


def _mod_kernel(c_ref, w_ref, b_ref, o_ref):
    s = jax.nn.silu(c_ref[...])
    o_ref[0] = _dot3(s, w_ref[0]) + b_ref[0]


def _modulation(cond8, w_mod, b_mod):
    depth, d, n = w_mod.shape
    tn = 1536
    return pl.pallas_call(
        _mod_kernel,
        grid=(depth, n // tn),
        in_specs=[pl.BlockSpec((8, d), lambda l, j: (0, 0)),
                  pl.BlockSpec((1, d, tn), lambda l, j: (l, 0, j)),
                  pl.BlockSpec((1, 1, tn), lambda l, j: (l, 0, j))],
        out_specs=pl.BlockSpec((1, 8, tn), lambda l, j: (l, 0, j)),
        out_shape=jax.ShapeDtypeStruct((depth, 8, n), F32),
        compiler_params=_cparams(("arbitrary", "arbitrary")),
        name="modulation",
    )(cond8, w_mod, b_mod.reshape(depth, 1, n))


def _normlin_kernel(x_ref, g_ref, mod_ref, w_ref, *rest, shift_idx, scale_idx, qk_tiles):
    if qk_tiles:
        bd_ref, qg_ref, o_ref, h_ref = rest
    else:
        o_ref, h_ref = rest
    j = pl.program_id(1)

    @pl.when(j == 0)
    def _():
        m = mod_ref[0]
        h = _modnorm(x_ref[...], g_ref[...], m[shift_idx:shift_idx + 1], m[scale_idx:scale_idx + 1])
        h_ref[...] = h.astype(BF16)

    acc = jnp.dot(h_ref[...], w_ref[...], preferred_element_type=F32)

    if qk_tiles:
        @pl.when(j < qk_tiles)
        def _():
            sh, sl = _split2(acc * acc)
            bd = bd_ref[...]
            ms = jnp.dot(sh, bd, preferred_element_type=F32) + jnp.dot(sl, bd, preferred_element_type=F32)
            o_ref[...] = acc * lax.rsqrt(ms + EPS) * qg_ref[...]

        @pl.when(j >= qk_tiles)
        def _():
            o_ref[...] = acc
    else:
        o_ref[...] = acc


def _normlin(x, g, mod, w, row_fn, tm, tn, shift_idx, scale_idx, qk=None):
    ntok, d = x.shape
    n = w.shape[1]
    in_specs = [pl.BlockSpec((tm, d), lambda i, j: (i, 0)),
                pl.BlockSpec((1, d), lambda i, j: (0, 0)),
                pl.BlockSpec((1, N_MOD, d), lambda i, j: (row_fn(i), 0, 0)),
                pl.BlockSpec((d, tn), lambda i, j: (0, j))]
    args = [x, g.reshape(1, d), mod, w.astype(BF16)]
    qk_tiles = 0
    if qk is not None:
        bd, qg, n_qk = qk
        qk_tiles = n_qk // tn
        in_specs += [pl.BlockSpec((tn, tn), lambda i, j: (0, 0)),
                     pl.BlockSpec((1, tn), lambda i, j: (0, jnp.minimum(j, qk_tiles - 1)))]
        args += [bd, qg]
    return pl.pallas_call(
        functools.partial(_normlin_kernel, shift_idx=shift_idx, scale_idx=scale_idx, qk_tiles=qk_tiles),
        grid=(ntok // tm, n // tn),
        in_specs=in_specs,
        out_specs=pl.BlockSpec((tm, tn), lambda i, j: (i, j)),
        out_shape=jax.ShapeDtypeStruct((ntok, n), F32),
        scratch_shapes=[pltpu.VMEM((tm, d), BF16)],
        compiler_params=_cparams(("arbitrary", "arbitrary")),
        name="normlin",
    )(*args)


GLA_HEADS_PER_STEP = 2


def _gla_kernel(tbl_ref, zq_ref, zf_ref, zv_ref, lb_ref, s0_ref, o_ref, sfin_ref, st_ref, *, reverse, layer_j):
    step = pl.program_id(1)
    tb = zq_ref.shape[0]
    nchunk = tb // GLA_CHUNK
    nh = zq_ref.shape[1] // A_DK

    @pl.when((tbl_ref[2, step] == 1) & (tbl_ref[4, step] == 1))
    def _():
        for hh in range(nh):
            st_ref[hh] = s0_ref[0, 0, hh].T

    @pl.when((tbl_ref[2, step] == 1) & (tbl_ref[4, step] == 0))
    def _():
        st_ref[...] = jnp.zeros_like(st_ref)

    row = lax.broadcasted_iota(jnp.int32, (tb, tb), 0)
    col = lax.broadcasted_iota(jnp.int32, (tb, tb), 1)
    same = (row // GLA_CHUNK) == (col // GLA_CHUNK)
    if reverse:
        incl = same & (col >= row)
        rest = same & (col < row)
    else:
        incl = same & (col <= row)
        rest = same & (col > row)
    incl_m = jnp.where(incl, 1.0, 0.0).astype(BF16)
    rest_m = jnp.where(rest, 1.0, 0.0).astype(BF16)
    ridx = lax.broadcasted_iota(jnp.int32, (tb, A_DK), 0) // GLA_CHUNK
    cidx_l = lax.broadcasted_iota(jnp.int32, (A_DV, tb), 1) // GLA_CHUNK

    lbx = lb_ref[0]
    e = jnp.exp(lbx - jnp.max(lbx, axis=0, keepdims=True))
    p = e / jnp.sum(e, axis=0, keepdims=True)
    lb_all = jnp.zeros((1, lbx.shape[1]), F32)
    for i in range(1, layer_j + 1):
        lb_all = lb_all + p[i:i + 1]

    for hh in range(nh):
        sl = slice(hh * A_DK, (hh + 1) * A_DK)
        lb = lb_all[:, sl]
        z = zf_ref[:, sl]
        log_sig = jnp.minimum(z, 0.0) - jnp.log1p(jnp.exp(-jnp.abs(z)))
        a_ = jnp.log1p(-lb) + log_sig
        b_ = jnp.log(lb)
        logf = jnp.maximum(a_, b_) + jnp.log1p(jnp.exp(-jnp.abs(a_ - b_)))
        k = (1.0 - lb) * jax.nn.sigmoid(-z)
        q = zq_ref[:, sl] * (A_DK ** -0.5)
        v = zv_ref[:, sl]

        bc = _exact_left(incl_m, logf)
        brem = _exact_left(rest_m, logf)
        qd = q * jnp.exp(bc)
        kinv = k * jnp.exp(-bc)
        att = jnp.where(incl, _bdot_nt(qd, kinv), 0.0)
        o_intra = _bdot(att, v)
        klast = k * jnp.exp(brem)
        vt = v.T
        order = list(range(nchunk - 1, -1, -1) if reverse else range(nchunk))
        upd = []
        for pos in range(0, nchunk, 2):
            w2 = jnp.concatenate([jnp.where(ridx == order[pos], klast, 0.0),
                                  jnp.where(ridx == order[pos + 1], klast, 0.0)], axis=1)
            u2 = _bdot(vt, w2)
            upd += [u2[:, :A_DK], u2[:, A_DK:]]
        st = st_ref[hh]
        starts = []
        for pos, c in enumerate(order):
            starts.append(st)
            last = c * GLA_CHUNK if reverse else (c + 1) * GLA_CHUNK - 1
            a_c = jnp.exp(bc[last:last + 1])
            st = a_c * st + upd[pos]
        qdt = qd.T
        qdt_blocks = jnp.concatenate([jnp.where(cidx_l == c, qdt, 0.0) for c in order], axis=0)
        o_inter = _bdot(jnp.concatenate(starts, axis=1), qdt_blocks).T
        st_ref[hh] = st
        o_ref[:, sl] = o_intra + o_inter
        sfin_ref[0, hh] = st.T


def _gla(z, hg_lb_t, s0, tbl, nseq, *, reverse, layer_j, qcol, fcol, vcol):
    ntok = z.shape[0]
    nsteps = tbl.shape[1]
    tb = TOK_BLK
    n_rec = hg_lb_t.shape[1]
    d = 1 if reverse else 0
    nh = GLA_HEADS_PER_STEP
    w = nh * A_DK
    qb, fb, vb = qcol * A_DK // w, fcol * A_DK // w, vcol * A_DK // w
    grid_spec = pltpu.PrefetchScalarGridSpec(
        num_scalar_prefetch=1,
        grid=(A_HEADS // nh, nsteps),
        in_specs=[pl.BlockSpec((tb, w), lambda h, s, t: (t[0, s], qb + h)),
                  pl.BlockSpec((tb, w), lambda h, s, t: (t[0, s], fb + h)),
                  pl.BlockSpec((tb, w), lambda h, s, t: (t[0, s], vb + h)),
                  pl.BlockSpec((1, n_rec, w), lambda h, s, t: (d, 0, h)),
                  pl.BlockSpec((1, 1, nh, A_DK, A_DV), lambda h, s, t: (t[3, s], layer_j, h, 0, 0))],
        out_specs=[pl.BlockSpec((tb, w), lambda h, s, t: (t[0, s], h)),
                   pl.BlockSpec((1, nh, A_DK, A_DV), lambda h, s, t: (t[1, s], h, 0, 0))],
        scratch_shapes=[pltpu.VMEM((nh, A_DV, A_DK), F32)])
    return pl.pallas_call(
        functools.partial(_gla_kernel, reverse=reverse, layer_j=layer_j),
        grid_spec=grid_spec,
        out_shape=[jax.ShapeDtypeStruct((ntok, A_HEADS * A_DV), F32),
                   jax.ShapeDtypeStruct((nseq, A_HEADS, A_DK, A_DV), F32)],
        compiler_params=_cparams(("arbitrary", "arbitrary")),
        name="gla_bwd" if reverse else "gla_fwd",
    )(tbl, z, z, z, hg_lb_t, s0)


def _conv_kernel(tbl_ref, ua_ref, ug_ref, pa_ref, pg_ref, na_ref, ng_ref, w_ref, b_ref, lg_ref, lbias_ref, o_ref,
                 buf_ref):
    i = pl.program_id(0)
    tb = ua_ref.shape[0]
    hl = CONV_HALO
    prev_ok = tbl_ref[0, i] == 1
    next_ok = tbl_ref[1, i] == 1
    buf_ref[hl:hl + tb, :] = ua_ref[...] * jax.nn.sigmoid(ug_ref[...])
    buf_ref[0:hl, :] = jnp.where(prev_ok, pa_ref[...] * jax.nn.sigmoid(pg_ref[...]), 0.0)
    buf_ref[hl + tb:hl + tb + hl, :] = jnp.where(next_ok, na_ref[...] * jax.nn.sigmoid(ng_ref[...]), 0.0)
    w = w_ref[...]
    acc = jnp.zeros(o_ref.shape, F32) + b_ref[...]
    base = hl - CONV_W // 2
    for t in range(CONV_W):
        acc = acc + buf_ref[base + t:base + t + tb, :] * w[t:t + 1]
    mu = jnp.mean(acc, axis=-1, keepdims=True)
    xc = acc - mu
    var = jnp.mean(xc * xc, axis=-1, keepdims=True)
    y = xc * lax.rsqrt(var + EPS) * lg_ref[...] + lbias_ref[...]
    o_ref[...] = jax.nn.silu(y)


def _conv(z, tbl, cv_w, cv_b, ln_g, ln_b, *, acol, gcol):
    ntok = z.shape[0]
    ch = cv_w.shape[1]
    tb = TOK_BLK
    hl = CONV_HALO
    r = tb // hl
    last = ntok // hl - 1
    prev = lambda i, t: jnp.maximum(i * r - 1, 0)
    nxt = lambda i, t: jnp.minimum((i + 1) * r, last)
    grid_spec = pltpu.PrefetchScalarGridSpec(
        num_scalar_prefetch=1,
        grid=(ntok // tb,),
        in_specs=[pl.BlockSpec((tb, ch), lambda i, t: (i, acol)),
                  pl.BlockSpec((tb, ch), lambda i, t: (i, gcol)),
                  pl.BlockSpec((hl, ch), lambda i, t: (prev(i, t), acol)),
                  pl.BlockSpec((hl, ch), lambda i, t: (prev(i, t), gcol)),
                  pl.BlockSpec((hl, ch), lambda i, t: (nxt(i, t), acol)),
                  pl.BlockSpec((hl, ch), lambda i, t: (nxt(i, t), gcol)),
                  pl.BlockSpec((CONV_W, ch), lambda i, t: (0, 0)),
                  pl.BlockSpec((1, ch), lambda i, t: (0, 0)),
                  pl.BlockSpec((1, ch), lambda i, t: (0, 0)),
                  pl.BlockSpec((1, ch), lambda i, t: (0, 0))],
        out_specs=pl.BlockSpec((tb, ch), lambda i, t: (i, 0)),
        scratch_shapes=[pltpu.VMEM((tb + 2 * hl, ch), F32)])
    return pl.pallas_call(
        _conv_kernel,
        grid_spec=grid_spec,
        out_shape=jax.ShapeDtypeStruct((ntok, ch), F32),
        compiler_params=_cparams(("arbitrary",)),
        name="conv_module",
    )(tbl, z, z, z, z, z, z, cv_w, cv_b.reshape(1, ch), ln_g.reshape(1, ch), ln_b.reshape(1, ch))


def _even_out_kernel(x_ref, mod_ref, of_ref, ob_ref, zg_ref, oc_ref, gn_ref, w_ref, y_ref, *, gate_idx):
    o = of_ref[...] + ob_ref[...]
    gate = jax.nn.silu(zg_ref[...])
    gn = gn_ref[...]
    parts = []
    for h in range(A_HEADS):
        oh = o[:, h * A_DV:(h + 1) * A_DV]
        ms = jnp.mean(oh * oh, axis=-1, keepdims=True)
        parts.append(oh * lax.rsqrt(ms + EPS) * gn * gate[:, h * A_DV:(h + 1) * A_DV])
    o_a = jnp.concatenate(parts, axis=-1)
    na = o_a.shape[1]
    acc = _bdot(o_a, w_ref[0:na, :]) + _bdot(oc_ref[...], w_ref[na:, :])
    g = mod_ref[0][gate_idx:gate_idx + 1]
    y_ref[...] = x_ref[...] + g * acc


def _even_out(x, mod, o_f, o_b, z, o_c, gn_g, w_out, row_fn, tm, gate_idx, gcol):
    ntok, d = x.shape
    na = o_f.shape[1]
    nc = o_c.shape[1]
    return pl.pallas_call(
        functools.partial(_even_out_kernel, gate_idx=gate_idx),
        grid=(ntok // tm,),
        in_specs=[pl.BlockSpec((tm, d), lambda i: (i, 0)),
                  pl.BlockSpec((1, N_MOD, d), lambda i: (row_fn(i), 0, 0)),
                  pl.BlockSpec((tm, na), lambda i: (i, 0)),
                  pl.BlockSpec((tm, na), lambda i: (i, 0)),
                  pl.BlockSpec((tm, na), lambda i: (i, gcol)),
                  pl.BlockSpec((tm, nc), lambda i: (i, 0)),
                  pl.BlockSpec((1, A_DV), lambda i: (0, 0)),
                  pl.BlockSpec((na + nc, d), lambda i: (0, 0))],
        out_specs=pl.BlockSpec((tm, d), lambda i: (i, 0)),
        out_shape=jax.ShapeDtypeStruct((ntok, d), F32),
        compiler_params=_cparams(("arbitrary",)),
        name="even_out",
    )(x, mod, o_f, o_b, z, o_c, gn_g.reshape(1, A_DV), w_out.astype(BF16))


def _lin_out_kernel(x_ref, mod_ref, o_ref, w_ref, y_ref, *, gate_idx):
    g = mod_ref[0][gate_idx:gate_idx + 1]
    y_ref[...] = x_ref[...] + g * _bdot(o_ref[...], w_ref[...])


def _lin_out(x, mod, o, w, row_fn, tm, gate_idx):
    ntok, d = x.shape
    kdim = o.shape[1]
    return pl.pallas_call(
        functools.partial(_lin_out_kernel, gate_idx=gate_idx),
        grid=(ntok // tm,),
        in_specs=[pl.BlockSpec((tm, d), lambda i: (i, 0)),
                  pl.BlockSpec((1, N_MOD, d), lambda i: (row_fn(i), 0, 0)),
                  pl.BlockSpec((tm, kdim), lambda i: (i, 0)),
                  pl.BlockSpec((kdim, d), lambda i: (0, 0))],
        out_specs=pl.BlockSpec((tm, d), lambda i: (i, 0)),
        out_shape=jax.ShapeDtypeStruct((ntok, d), F32),
        compiler_params=_cparams(("arbitrary",)),
        name="lin_out",
    )(x, mod, o, w.astype(BF16))


def _ctx_attn_kernel(q_ref, k_ref, v_ref, o_ref):
    scale = C_HD ** -0.5
    for h in range(C_HEADS):
        sl = slice(h * C_HD, (h + 1) * C_HD)
        s = _bdot_nt(q_ref[:, sl], k_ref[:, sl]) * scale
        e = jnp.exp(s - jnp.max(s, axis=-1, keepdims=True))
        p = e / jnp.sum(e, axis=-1, keepdims=True)
        o_ref[:, sl] = _bdot(p, v_ref[:, sl])


def _ctx_attn(zqkv, n_prompt, seq):
    ntok = zqkv.shape[0]
    w = C_HEADS * C_HD
    return pl.pallas_call(
        _ctx_attn_kernel,
        grid=(n_prompt // seq,),
        in_specs=[pl.BlockSpec((seq, w), lambda b: (b, 0)),
                  pl.BlockSpec((seq, w), lambda b: (b, 1)),
                  pl.BlockSpec((seq, w), lambda b: (b, 2))],
        out_specs=pl.BlockSpec((seq, w), lambda b: (b, 0)),
        out_shape=jax.ShapeDtypeStruct((ntok, w), F32),
        compiler_params=_cparams(("arbitrary",)),
        name="ctx_attn",
    )(zqkv, zqkv, zqkv)


NA_QROWS = 8
NA_KROWS = 16


def _na_blocks(rows):
    kh = min(WIN_H, rows)
    assert kh == WIN_H and rows % NA_QROWS == 0 and rows >= NA_KROWS
    blocks, types = [], []
    for m in range(rows // NA_QROWS):
        k0 = int(np.clip(NA_QROWS * m - kh // 2, 0, rows - NA_KROWS))
        tbl = np.full((NA_QROWS, NA_KROWS), 2 * WIN_H - 1, np.int64)
        for ql in range(NA_QROWS):
            rq = NA_QROWS * m + ql
            r0 = int(np.clip(rq - kh // 2, 0, rows - kh))
            for kl in range(NA_KROWS):
                rk = k0 + kl
                if r0 <= rk < r0 + kh:
                    tbl[ql, kl] = rk - rq + (WIN_H - 1)
        key = tbl.tobytes()
        if key not in [t.tobytes() for t in types]:
            types.append(tbl)
        blocks.append((NA_QROWS * m, k0, [t.tobytes() for t in types].index(key)))
    return blocks, types


def _na_kernel(o_in_ref, q_ref, k_ref, v_ref, kc_ref, vc_ref, m_ref, o_ref, bias_ref, *, blocks, types):
    del o_in_ref
    scale = C_HD ** -0.5

    @pl.when(pl.program_id(1) == 0)
    def _():
        for hh in range(LANES // C_HD):
            for t, tbl in enumerate(types):
                for ql in range(NA_QROWS):
                    for kl in range(0, NA_KROWS, 2):
                        pair = jnp.concatenate([m_ref[hh, int(tbl[ql, kl])], m_ref[hh, int(tbl[ql, kl + 1])]], axis=1)
                        bias_ref[hh, t, ql * GRID_W:(ql + 1) * GRID_W, kl * GRID_W:(kl + 2) * GRID_W] = pair

    for hh in range(LANES // C_HD):
        sl = slice(hh * C_HD, (hh + 1) * C_HD)
        kc = kc_ref[0, 0, hh]
        vc = vc_ref[0, 0, hh]
        for q0, k0, t in blocks:
            qs = slice(q0 * GRID_W, (q0 + NA_QROWS) * GRID_W)
            ks = slice(k0 * GRID_W, (k0 + NA_KROWS) * GRID_W)
            qb = q_ref[qs, sl]
            s_loc = _bdot_nt(qb, k_ref[ks, sl]) * scale + bias_ref[hh, t]
            s_ctx = _bdot_nt(qb, kc) * scale
            m = jnp.maximum(jnp.max(s_loc, axis=-1, keepdims=True), jnp.max(s_ctx, axis=-1, keepdims=True))
            e_loc = jnp.exp(s_loc - m)
            e_ctx = jnp.exp(s_ctx - m)
            den = jnp.sum(e_loc, axis=-1, keepdims=True) + jnp.sum(e_ctx, axis=-1, keepdims=True)
            o = _bdot(e_loc, v_ref[ks, sl]) + _bdot(e_ctx, vc)
            o_ref[qs, sl] = o / den


def _na_attn(o_prompt, zqkv, cache_k, cache_v, tiles, blocks, types, layer_j, n_prompt, dec_seq):
    ntok = zqkv.shape[0]
    nb = (ntok - n_prompt) // dec_seq
    boff = n_prompt // dec_seq
    hp = LANES // C_HD
    npair = C_HEADS // hp
    past = cache_k.shape[3]
    return pl.pallas_call(
        functools.partial(_na_kernel, blocks=blocks, types=types),
        grid=(npair, nb),
        in_specs=[pl.BlockSpec(memory_space=pl.ANY),
                  pl.BlockSpec((dec_seq, LANES), lambda p, b: (boff + b, p)),
                  pl.BlockSpec((dec_seq, LANES), lambda p, b: (boff + b, npair + p)),
                  pl.BlockSpec((dec_seq, LANES), lambda p, b: (boff + b, 2 * npair + p)),
                  pl.BlockSpec((1, 1, hp, past, C_HD), lambda p, b: (b, layer_j, p, 0, 0)),
                  pl.BlockSpec((1, 1, hp, past, C_HD), lambda p, b: (b, layer_j, p, 0, 0)),
                  pl.BlockSpec((hp,) + tiles.shape[1:], lambda p, b: (p, 0, 0, 0))],
        out_specs=pl.BlockSpec((dec_seq, LANES), lambda p, b: (boff + b, p)),
        out_shape=jax.ShapeDtypeStruct(o_prompt.shape, F32),
        scratch_shapes=[pltpu.VMEM((hp, len(types), NA_QROWS * GRID_W, NA_KROWS * GRID_W), F32)],
        input_output_aliases={0: 0},
        compiler_params=_cparams(("arbitrary", "arbitrary")),
        name="na_attn",
    )(o_prompt, zqkv, zqkv, zqkv, cache_k, cache_v, tiles)


def _na_bias_kernel(rpb_ref, e_ref, neg_ref, o_ref):
    h, m, l = _split3(rpb_ref[...])
    e = e_ref[...]
    d = lambda y: jnp.dot(y, e, preferred_element_type=F32)
    o_ref[...] = d(h) + (d(m) + d(l)) + neg_ref[...]


def _na_bias_tiles(rpb_l):
    nh, ndr, ndc = rpb_l.shape
    cq = np.arange(GRID_W)
    c0 = np.clip(cq - WIN_W // 2, 0, GRID_W - WIN_W)
    in_win = (cq[None, :] >= c0[:, None]) & (cq[None, :] < c0[:, None] + WIN_W)
    dc = np.clip(cq[None, :] - cq[:, None] + (WIN_W - 1), 0, ndc - 1)
    ndc_pad = 32
    onehot = (dc[None] == np.arange(ndc_pad)[:, None, None]) & in_win[None]
    onehot = jnp.asarray(onehot.reshape(ndc_pad, GRID_W * GRID_W), BF16)
    neg = jnp.asarray(np.where(in_win, 0.0, -np.inf).reshape(1, GRID_W * GRID_W), F32)
    rpb2 = jnp.pad(rpb_l.astype(F32).reshape(nh * ndr, ndc), ((0, 0), (0, ndc_pad - ndc)))
    m = pl.pallas_call(
        _na_bias_kernel,
        out_shape=jax.ShapeDtypeStruct((nh * ndr, GRID_W * GRID_W), F32),
        name="na_bias",
    )(rpb2, onehot, neg).reshape(nh, ndr, GRID_W, GRID_W)
    return jnp.concatenate([m, jnp.full((nh, 1, GRID_W, GRID_W), -jnp.inf, F32)], axis=1)


def _top_desc(x, n):
    vals = []
    for _ in range(n):
        m = jnp.max(x, axis=0, keepdims=True)
        vals.append(m)
        x = jnp.where(x == m, -jnp.inf, x)
    return vals


SUBLANES = 8


def _sort_network(n):
    pairs = []
    p = 1
    while p < n:
        k = p
        while k >= 1:
            for j in range(k % p, n - k, 2 * k):
                for i in range(min(k, n - j - k)):
                    if (i + j) // (2 * p) == (i + j + k) // (2 * p):
                        pairs.append((i + j, i + j + k))
            k //= 2
        p *= 2
    return pairs


def _top_rows(x, n):
    g = [x[SUBLANES * k:SUBLANES * (k + 1)] for k in range(n)]
    for i, j in _sort_network(n):
        g[i], g[j] = jnp.maximum(g[i], g[j]), jnp.minimum(g[i], g[j])
    shift = SUBLANES // 2
    while shift >= 1:
        other = [pltpu.roll(a, shift, axis=0) for a in g]
        g = [jnp.maximum(g[k], other[n - 1 - k]) for k in range(n)]
        stride = n // 2
        while stride >= 1:
            for k in range(n):
                if k & stride == 0:
                    g[k], g[k + stride] = jnp.maximum(g[k], g[k + stride]), jnp.minimum(g[k], g[k + stride])
            stride //= 2
        shift //= 2
    tops = [a[0:1] for a in g]
    nxt = jnp.max(jnp.where(x < tops[n - 1], x, -jnp.inf), axis=0, keepdims=True)
    return tops + [nxt]


def _peer_gate_kernel(x_ref, g_ref, mod_ref, w_ref, keys_ref, a_ref, b_ref, tau_ref, hn_ref, hq_ref, *,
                      shift_idx, scale_idx):
    kq = PEER_TOPK
    m = mod_ref[0]
    hn = _modnorm(x_ref[...], g_ref[...], m[shift_idx:shift_idx + 1], m[scale_idx:scale_idx + 1])
    hn_ref[...] = hn.astype(BF16)

    def query(h):
        cols = slice(h * 2 * N_KEYS, (h + 1) * 2 * N_KEYS)
        hq_ref[h % 2] = jnp.dot(hn_ref[...], w_ref[:, cols], preferred_element_type=F32)

    query(0)
    for h in range(PEER_HEADS):
        if h + 1 < PEER_HEADS:
            query(h + 1)
        hq = hq_ref[h % 2]
        s = []
        tops = []
        for p in range(2):
            sp = _dot3_nt(keys_ref[h, p], hq[:, p * N_KEYS:(p + 1) * N_KEYS])
            s.append(sp)
            tops.append(_top_rows(sp, kq))
        t0 = jnp.concatenate(tops[0][:kq], axis=0)
        t1 = jnp.concatenate(tops[1][:kq], axis=0)
        ninf = jnp.full((4, t0.shape[1]), -jnp.inf, F32)
        cand = jnp.concatenate([
            tops[0][0] + t1,
            tops[0][1] + t1[0:8],
            tops[0][2] + t1[0:8],
            tops[0][3] + t1[0:8],
            t0[8:16] + tops[1][0],
            t0[4:8] + tops[1][0], t0[4:8] + tops[1][1],
            t0[4:8] + tops[1][2], ninf], axis=0)
        best = _top_desc(cand, kq + 1)
        mx = best[0]
        z = jnp.zeros_like(mx)
        for r in range(kq):
            z = z + jnp.exp(best[r] - mx)
        nxt = jnp.maximum(best[kq], jnp.maximum(tops[0][kq] + tops[1][0], tops[0][0] + tops[1][kq]))
        thr = 0.5 * (best[kq - 1] + nxt)
        inv_z = 1.0 / z
        a_ref[h] = jnp.exp(s[0] - tops[0][0]) * inv_z
        b_ref[h] = jnp.exp(s[1] - tops[1][0]).astype(BF16)
        tau_ref[h:h + 1, :] = jnp.exp(thr - mx) * inv_z


def _peer_gates(x, g, mod, wq, keys, row_fn, tmk, shift_idx, scale_idx):
    ntok, d = x.shape
    nq = wq.shape[1]
    return pl.pallas_call(
        functools.partial(_peer_gate_kernel, shift_idx=shift_idx, scale_idx=scale_idx),
        grid=(ntok // tmk,),
        in_specs=[pl.BlockSpec((tmk, d), lambda i: (i, 0)),
                  pl.BlockSpec((1, d), lambda i: (0, 0)),
                  pl.BlockSpec((1, N_MOD, d), lambda i: (row_fn(i), 0, 0)),
                  pl.BlockSpec((d, nq), lambda i: (0, 0)),
                  pl.BlockSpec(keys.shape, lambda i: (0, 0, 0, 0))],
        out_specs=[pl.BlockSpec((PEER_HEADS, N_KEYS, tmk), lambda i: (0, 0, i)),
                   pl.BlockSpec((PEER_HEADS, N_KEYS, tmk), lambda i: (0, 0, i)),
                   pl.BlockSpec((PEER_HEADS, tmk), lambda i: (0, i))],
        out_shape=[jax.ShapeDtypeStruct((PEER_HEADS, N_KEYS, ntok), F32),
                   jax.ShapeDtypeStruct((PEER_HEADS, N_KEYS, ntok), BF16),
                   jax.ShapeDtypeStruct((PEER_HEADS, ntok), F32)],
        scratch_shapes=[pltpu.VMEM((tmk, d), BF16),
                        pltpu.VMEM((2, tmk, 2 * N_KEYS), F32)],
        compiler_params=_cparams(("arbitrary",)),
        name="peer_gates",
    )(x, g.reshape(1, d), mod, wq.astype(BF16), keys)


BF16_ROWS = 16
PEER_TOK_SUB = 256


def _peer_mix_kernel(x_ref, g_ref, mod_ref, u0_ref, u_ref, vt_ref, a_ref, b_ref, tau_ref, y_ref, h_ref, acc_ref,
                     act_ref, ht_ref, *, shift_idx, scale_idx, gate_idx):
    e = pl.program_id(1)
    te = u_ref.shape[0]
    tm = x_ref.shape[0]
    nsub = te // N_KEYS
    nq = tm // PEER_TOK_SUB
    cols = [slice(q * PEER_TOK_SUB, (q + 1) * PEER_TOK_SUB) for q in range(nq)]

    def activations(u, cs):
        act_ref[:, cs] = jnp.dot(u[...], h_ref[:, cs], preferred_element_type=F32)

    @pl.when(e == 0)
    def _():
        m = mod_ref[0]
        h = _modnorm(x_ref[...], g_ref[...], m[shift_idx:shift_idx + 1], m[scale_idx:scale_idx + 1])
        h_ref[...] = h.T.astype(BF16)
        acc_ref[...] = jnp.zeros_like(acc_ref)
        for cs in cols:
            activations(u0_ref, cs)

    grp = (N_KEYS // BF16_ROWS, BF16_ROWS, PEER_TOK_SUB)
    bcast = lambda r: jnp.broadcast_to(r, (BF16_ROWS, PEER_TOK_SUB)).astype(BF16)[None]
    zero = jnp.zeros(grp, BF16)
    for q, cs in enumerate(cols):
        taus = [bcast(tau_ref[h:h + 1, cs]) for h in range(PEER_HEADS)]
        for ii in range(nsub):
            i = e * nsub + ii
            rows = slice(ii * N_KEYS, (ii + 1) * N_KEYS)
            gsum = None
            for h in range(PEER_HEADS):
                w = bcast(a_ref[h, pl.ds(i, 1), cs]) * b_ref[h, :, cs].reshape(grp)
                term = jnp.where(w >= taus[h], w, zero)
                gsum = term if gsum is None else gsum + term
            act = jax.nn.gelu(act_ref[rows, cs].astype(BF16)).reshape(grp)
            ht_ref[q % 2, rows, :] = (act * gsum).reshape(N_KEYS, PEER_TOK_SUB)
        activations(u_ref, cs)
        acc_ref[:, cs] += jnp.dot(vt_ref[...], ht_ref[q % 2], preferred_element_type=F32)

    @pl.when(e == pl.num_programs(1) - 1)
    def _():
        g = mod_ref[0][gate_idx:gate_idx + 1]
        y_ref[...] = x_ref[...] + g * acc_ref[...].T


def _peer_mix(x, g, mod, u_bf, vt_bf, a, b, tau, row_fn, tm, te, shift_idx, scale_idx, gate_idx):
    ntok, d = x.shape
    nexp = u_bf.shape[0]
    last = nexp // te - 1
    return pl.pallas_call(
        functools.partial(_peer_mix_kernel, shift_idx=shift_idx, scale_idx=scale_idx, gate_idx=gate_idx),
        grid=(ntok // tm, nexp // te),
        in_specs=[pl.BlockSpec((tm, d), lambda t, e: (t, 0)),
                  pl.BlockSpec((1, d), lambda t, e: (0, 0)),
                  pl.BlockSpec((1, N_MOD, d), lambda t, e: (row_fn(t), 0, 0)),
                  pl.BlockSpec((te, d), lambda t, e: (0, 0)),
                  pl.BlockSpec((te, d), lambda t, e: (jnp.minimum(e + 1, last), 0)),
                  pl.BlockSpec((d, te), lambda t, e: (0, e)),
                  pl.BlockSpec((PEER_HEADS, N_KEYS, tm), lambda t, e: (0, 0, t)),
                  pl.BlockSpec((PEER_HEADS, N_KEYS, tm), lambda t, e: (0, 0, t)),
                  pl.BlockSpec((PEER_HEADS, tm), lambda t, e: (0, t))],
        out_specs=pl.BlockSpec((tm, d), lambda t, e: (t, 0)),
        out_shape=jax.ShapeDtypeStruct((ntok, d), F32),
        scratch_shapes=[pltpu.VMEM((d, tm), BF16),
                        pltpu.VMEM((d, tm), F32),
                        pltpu.VMEM((te, tm), F32),
                        pltpu.VMEM((2, te, PEER_TOK_SUB), BF16)],
        compiler_params=_cparams(("arbitrary", "arbitrary")),
        name="peer_mix",
    )(x, g.reshape(1, d), mod, u_bf, u_bf, vt_bf, a, b, tau)


def _seq_tables(n_prompt, seq, n_dec, dec_seq, tb):
    seq_of, first, last, cached, has = [], [], [], [], []
    sid = 0
    for count, length, is_dec in ((n_prompt // seq, seq, 0), (n_dec, dec_seq, 1)):
        nb = length // tb
        for k in range(count):
            for b in range(nb):
                seq_of.append(sid)
                first.append(int(b == 0))
                last.append(int(b == nb - 1))
                cached.append(k if is_dec else 0)
                has.append(is_dec)
            sid += 1
    nblk = len(seq_of)
    fwd = np.array([list(range(nblk)), seq_of, first, cached, has], np.int32)
    order = list(range(nblk - 1, -1, -1))
    pick = lambda a: [a[i] for i in order]
    bwd = np.array([order, pick(seq_of), pick(last), pick(cached), pick(has)], np.int32)
    halo = np.array([[1 - f for f in first], [1 - l for l in last]], np.int32)
    return fwd, bwd, halo, sid


def kernel(x_prompt, x_sample, cache_k, cache_v, state_fwd, state_bwd, c, c_ctx, norm_g, w_mod, b_mod, w_in_ab, hg_lb,
           hg_norm_g, cv_w, cv_b, cv_ln_g, cv_ln_b, w_out_ab, w_qkv, qk_g, rpb, w_out_na, peer_wq, peer_keys, peer_u,
           peer_v):
    bp, seq, d = x_prompt.shape
    nd, dec_seq, _ = x_sample.shape
    depth = w_mod.shape[0]
    n_prompt = bp * seq
    ntok = n_prompt + nd * dec_seq
    assert n_prompt % dec_seq == 0 and seq % TOK_BLK == 0 and dec_seq % TOK_BLK == 0
    rows = dec_seq // GRID_W
    n_qk = A_HEADS * A_DK
    n_v = A_HEADS * A_DV
    n_b = d - n_v

    tile = lambda limit: max(t for t in (128, 256, 512, 1024, 2048)
                             if t <= limit and n_prompt % t == 0 and dec_seq % t == 0)
    tm_lin = tile(2048)
    tm_out = tile(512)
    tm_peer = tile(1024)
    te_peer = 512
    tmk = 256

    x = jnp.concatenate([x_prompt.reshape(n_prompt, d), x_sample.reshape(nd * dec_seq, d)], axis=0)
    cond8 = jnp.zeros((8, d), F32).at[0].set(c_ctx).at[1:1 + nd].set(c)
    mods = _modulation(cond8, w_mod, b_mod).reshape(depth, 8, N_MOD, d)

    fwd_tbl, bwd_tbl, halo_tbl, nseq = _seq_tables(n_prompt, seq, nd, dec_seq, TOK_BLK)
    fwd_tbl, bwd_tbl, halo_tbl = jnp.asarray(fwd_tbl), jnp.asarray(bwd_tbl), jnp.asarray(halo_tbl)
    hg_lb_t = jnp.transpose(hg_lb, (1, 0, 2))

    row = lambda tm: _mod_row(n_prompt, dec_seq, tm)
    tn_qkv = 512
    bd = (jnp.arange(tn_qkv)[:, None] // C_HD == jnp.arange(tn_qkv)[None, :] // C_HD).astype(BF16) * (1.0 / C_HD)

    new_k, new_v, new_sf, new_sb = [], [], [], []
    for l in range(depth):
        j = l // 2
        mod = mods[l]
        if l % 2 == 0:
            z = _normlin(x, norm_g[l, 0], mod, w_in_ab[j], row(tm_lin), tm_lin, 512, 0, 1)
            blk = lambda col: col // A_DK
            o_f, sf = _gla(z, hg_lb_t, state_fwd, fwd_tbl, nseq, reverse=False, layer_j=j,
                           qcol=0, fcol=blk(n_qk), vcol=blk(3 * n_qk))
            o_b, sb = _gla(z, hg_lb_t, state_bwd, bwd_tbl, nseq, reverse=True, layer_j=j,
                           qcol=0, fcol=blk(2 * n_qk), vcol=blk(3 * n_qk))
            new_sf.append(sf[:bp])
            new_sb.append(sb[:bp])
            cbase = 3 * n_qk + 2 * n_v
            o_c = _conv(z, halo_tbl, cv_w[j], cv_b[j], cv_ln_g[j], cv_ln_b[j],
                        acol=cbase // n_b, gcol=(cbase + n_b) // n_b)
            x = _even_out(x, mod, o_f, o_b, z, o_c, hg_norm_g[j], w_out_ab[j], row(tm_out), tm_out, 2,
                          (3 * n_qk + n_v) // n_v)
        else:
            wq = C_HEADS * C_HD
            qg = jnp.concatenate([jnp.tile(qk_g[j, 0], C_HEADS), jnp.tile(qk_g[j, 1], C_HEADS)]).reshape(1, 2 * wq)
            z = _normlin(x, norm_g[l, 0], mod, w_qkv[j], row(tm_lin), tm_lin, tn_qkv, 0, 1, qk=(bd, qg, 2 * wq))
            kv = z[:n_prompt].reshape(bp, seq, 3, C_HEADS, C_HD)
            new_k.append(jnp.transpose(kv[:, :, 1], (0, 2, 1, 3)))
            new_v.append(jnp.transpose(kv[:, :, 2], (0, 2, 1, 3)))
            o = _ctx_attn(z, n_prompt, seq)
            na_blocks, na_types = _na_blocks(rows)
            o = _na_attn(o, z, cache_k, cache_v, _na_bias_tiles(rpb[j]), na_blocks, na_types, j, n_prompt, dec_seq)
            x = _lin_out(x, mod, o, w_out_na[j], row(tm_out), tm_out, 2)
        a, b, tau = _peer_gates(x, norm_g[l, 1], mod, peer_wq[l], peer_keys[l], row(tmk), tmk, 3, 4)
        x = _peer_mix(x, norm_g[l, 1], mod, peer_u[l].astype(BF16), peer_v[l].T.astype(BF16), a, b, tau,
                      row(tm_peer), tm_peer, te_peer, 3, 4, 5)

    yp = x[:n_prompt].reshape(bp, seq, d)
    ys = x[n_prompt:].reshape(nd, dec_seq, d)
    return (yp, ys, jnp.stack(new_k, axis=1), jnp.stack(new_v, axis=1),
            jnp.stack(new_sf, axis=1), jnp.stack(new_sb, axis=1))
```
